```python
import jax, jax.numpy as jnp
from jax import lax
import numpy as np

D_MODEL = 2048
BATCH = 4
SEQ = 8192
DEPTH = 1

POOL_WIDTH = D_MODEL // 2
POOL_WINDOWS = (2, 4, 8, 16)
POOL_GROUP = POOL_WIDTH // len(POOL_WINDOWS)
LRU_WIDTH = D_MODEL // 2
LRU_HEADS = 8
LRU_HEAD_DIM = LRU_WIDTH // LRU_HEADS
CONV_WIDTH = 4
LRU_C = 8.0
MIX_WIDTH = POOL_WIDTH + LRU_WIDTH
IN_PROJ_WIDTH = POOL_WIDTH + 2 * LRU_WIDTH
PEER_HEADS = 8
PEER_N_KEYS = 128
PEER_N_EXPERTS = PEER_N_KEYS * PEER_N_KEYS
PEER_D_KEY = 256
PEER_HALF = PEER_D_KEY // 2
PEER_TOPK = 16
PEER_CHUNK = 128
EPS = 1e-6

kernel_name = "hybrid_pool_rglru_peer_block"


def rmsnorm(x, g):
    xf = x.astype(jnp.float32)
    y = xf * lax.rsqrt(jnp.mean(xf * xf, axis=-1, keepdims=True) + EPS)
    return (y * g.astype(jnp.float32)).astype(x.dtype)


def pool_mixer(u, pool_w, pool_b, pool_scale):
    B, S, _ = u.shape
    uf = u.astype(jnp.float32).reshape(B, S, len(POOL_WINDOWS), POOL_GROUP)
    csum = jnp.cumsum(uf, axis=1)
    pos = jnp.arange(1, S + 1, dtype=jnp.float32)[None, :, None]
    outs = []
    for g, w in enumerate(POOL_WINDOWS):
        c = csum[:, :, g]
        c_prev = jnp.pad(c, ((0, 0), (w, 0), (0, 0)))[:, :S]
        mean = (c - c_prev) / jnp.minimum(pos, float(w))
        outs.append(mean - uf[:, :, g])
    d = jnp.stack(outs, axis=2).astype(u.dtype)
    y = jnp.einsum('bsgi,gij->bsgj', d, pool_w) + pool_b
    return y.reshape(B, S, POOL_WIDTH) * pool_scale


def _lru_combine(left, right):
    a1, b1 = left
    a2, b2 = right
    return a1 * a2, a2 * b1 + b2


def rg_lru_mixer(xb, gate, conv_w, conv_b, gate_a_w, gate_a_b, gate_x_w, gate_x_b, lru_lambda):
    B, S, _ = xb.shape
    xp = jnp.pad(xb, ((0, 0), (CONV_WIDTH - 1, 0), (0, 0)))
    xc = conv_b + xp[:, 0:S] * conv_w[0]
    for k in range(1, CONV_WIDTH):
        xc = xc + xp[:, k:k + S] * conv_w[k]
    xh = xc.reshape(B, S, LRU_HEADS, LRU_HEAD_DIM)
    r = jax.nn.sigmoid(jnp.einsum('bshi,hij->bshj', xh, gate_a_w) + gate_a_b).reshape(B, S, LRU_WIDTH)
    i = jax.nn.sigmoid(jnp.einsum('bshi,hij->bshj', xh, gate_x_w) + gate_x_b).reshape(B, S, LRU_WIDTH)
    log_a = (LRU_C * r.astype(jnp.float32)) * jax.nn.log_sigmoid(lru_lambda.astype(jnp.float32))
    a = jnp.exp(log_a)
    mult = jnp.sqrt(-jnp.expm1(2.0 * log_a))
    b = mult * (i * xc).astype(jnp.float32)
    _, h = lax.associative_scan(_lru_combine, (a, b), axis=1)
    return h.astype(xb.dtype) * jax.nn.gelu(gate)


def peer_ffn(z, peer_wq, keys1, keys2, peer_u, peer_v):
    B, S, D = z.shape
    zt = z.reshape(-1, PEER_CHUNK, D)
    k1 = keys1.astype(jnp.float32)
    k2 = keys2.astype(jnp.float32)

    def chunk(zc):
        q = (zc @ peer_wq).astype(jnp.float32).reshape(PEER_CHUNK, PEER_HEADS, 2, PEER_HALF)
        s1 = jnp.einsum('chd,hkd->chk', q[:, :, 0], k1)
        s2 = jnp.einsum('chd,hkd->chk', q[:, :, 1], k2)
        v1, i1 = lax.top_k(s1, PEER_TOPK)
        v2, i2 = lax.top_k(s2, PEER_TOPK)
        cand = (v1[..., :, None] + v2[..., None, :]).reshape(PEER_CHUNK, PEER_HEADS, PEER_TOPK * PEER_TOPK)
        cand_idx = (i1[..., :, None] * PEER_N_KEYS + i2[..., None, :]).reshape(PEER_CHUNK, PEER_HEADS, PEER_TOPK * PEER_TOPK)
        top_s, top_pos = lax.top_k(cand, PEER_TOPK)
        expert = jnp.take_along_axis(cand_idx, top_pos, axis=-1)
        gate = jax.nn.softmax(top_s, axis=-1)
        u = jnp.take(peer_u, expert, axis=0)
        act = jax.nn.gelu(jnp.einsum('chkd,cd->chk', u, zc))
        v = jnp.take(peer_v, expert, axis=0)
        w = (gate * act.astype(jnp.float32)).astype(zc.dtype)
        return jnp.einsum('chk,chkd->cd', w, v).astype(zc.dtype)

    return lax.map(chunk, zt).reshape(B, S, D)


def setup_inputs(seed: int = 0) -> dict:
    key = jax.random.key(seed)
    ks = jax.random.split(key, 24)
    L = DEPTH
    f32 = jnp.float32

    def nrm(k, shape, scale):
        return jax.random.normal(k, shape, f32) * scale

    a0 = jax.random.uniform(ks[12], (L, LRU_WIDTH), f32, 0.9, 0.999) ** (1.0 / LRU_C)
    lru_lambda = jnp.log(a0) - jnp.log1p(-a0)
    return {
        "x": nrm(ks[0], (BATCH, SEQ, D_MODEL), 1.0),
        "norm1_g": 1.0 + nrm(ks[1], (L, D_MODEL), 0.02),
        "w_in": nrm(ks[2], (L, D_MODEL, IN_PROJ_WIDTH), D_MODEL ** -0.5),
        "pool_w": nrm(ks[3], (L, len(POOL_WINDOWS), POOL_GROUP, POOL_GROUP), POOL_GROUP ** -0.5),
        "pool_b": nrm(ks[4], (L, len(POOL_WINDOWS), POOL_GROUP), 0.01),
        "pool_scale": 1.0 + nrm(ks[5], (L, POOL_WIDTH), 0.02),
        "conv_w": nrm(ks[6], (L, CONV_WIDTH, LRU_WIDTH), CONV_WIDTH ** -0.5),
        "conv_b": nrm(ks[7], (L, LRU_WIDTH), 0.01),
        "gate_a_w": nrm(ks[8], (L, LRU_HEADS, LRU_HEAD_DIM, LRU_HEAD_DIM), LRU_HEAD_DIM ** -0.5),
        "gate_a_b": nrm(ks[9], (L, LRU_HEADS, LRU_HEAD_DIM), 0.01),
        "gate_x_w": nrm(ks[10], (L, LRU_HEADS, LRU_HEAD_DIM, LRU_HEAD_DIM), LRU_HEAD_DIM ** -0.5),
        "gate_x_b": nrm(ks[11], (L, LRU_HEADS, LRU_HEAD_DIM), 0.01),
        "lru_lambda": lru_lambda,
        "w_out": nrm(ks[13], (L, MIX_WIDTH, D_MODEL), MIX_WIDTH ** -0.5),
        "norm2_g": 1.0 + nrm(ks[14], (L, D_MODEL), 0.02),
        "peer_wq": nrm(ks[15], (L, D_MODEL, PEER_HEADS * PEER_D_KEY), D_MODEL ** -0.5),
        "peer_keys1": nrm(ks[16], (L, PEER_HEADS, PEER_N_KEYS, PEER_HALF), PEER_HALF ** -0.5),
        "peer_keys2": nrm(ks[17], (L, PEER_HEADS, PEER_N_KEYS, PEER_HALF), PEER_HALF ** -0.5),
        "peer_u": nrm(ks[18], (L, PEER_N_EXPERTS, D_MODEL), D_MODEL ** -0.5),
        "peer_v": nrm(ks[19], (L, PEER_N_EXPERTS, D_MODEL), PEER_HEADS ** -0.5),
        "norm_f_g": 1.0 + nrm(ks[20], (D_MODEL,), 0.02),
    }


def reference(x, norm1_g, w_in, pool_w, pool_b, pool_scale, conv_w, conv_b, gate_a_w, gate_a_b,
              gate_x_w, gate_x_b, lru_lambda, w_out, norm2_g, peer_wq, peer_keys1, peer_keys2,
              peer_u, peer_v, norm_f_g):
    h = x
    for l in range(DEPTH):
        z = rmsnorm(h, norm1_g[l])
        proj = z @ w_in[l]
        u_pool = proj[..., :POOL_WIDTH]
        x_lru = proj[..., POOL_WIDTH:POOL_WIDTH + LRU_WIDTH]
        g_lru = proj[..., POOL_WIDTH + LRU_WIDTH:]
        y_pool = pool_mixer(u_pool, pool_w[l], pool_b[l], pool_scale[l])
        y_lru = rg_lru_mixer(x_lru, g_lru, conv_w[l], conv_b[l], gate_a_w[l], gate_a_b[l],
                             gate_x_w[l], gate_x_b[l], lru_lambda[l])
        y = jnp.concatenate([y_pool, y_lru], axis=-1)
        h = h + y @ w_out[l]
        h = h + peer_ffn(rmsnorm(h, norm2_g[l]), peer_wq[l], peer_keys1[l], peer_keys2[l],
                         peer_u[l], peer_v[l])
    return rmsnorm(h, norm_f_g)
```

```python
import functools

import jax
import jax.numpy as jnp
from jax import lax
from jax.experimental import pallas as pl
from jax.experimental.pallas import tpu as pltpu

EPS = 1e-6
POOL_WINDOWS = (2, 4, 8, 16)
POOL_TAIL = 16
CONV_WIDTH = 4
CONV_TAIL = 8
LRU_C = 8.0
LRU_HEADS = 8
PEER_HEADS = 8
PEER_TOPK = 16
PEER_N_KEYS = 128

LANES = 128
SUBLANES = 8

F32 = jnp.float32
BF16 = jnp.bfloat16


def _rmsnorm(x, g):
    return x * lax.rsqrt(jnp.mean(x * x, axis=-1, keepdims=True) + EPS) * g


def _gelu_tanh(x):
    c = 0.7978845608028654
    return x * (0.5 * (1.0 + jnp.tanh(c * (x + 0.044715 * (x * x * x)))))


def _norm_matmul_kernel(x_ref, g_ref, w_ref, o_ref):
    z = _rmsnorm(x_ref[...], g_ref[...])
    o_ref[...] = jnp.dot(z.astype(BF16), w_ref[...], preferred_element_type=F32)


def _norm_matmul(x, g, w, *, tm=512, tn=1024):
    m, d = x.shape
    n = w.shape[1]
    return pl.pallas_call(
        _norm_matmul_kernel,
        grid=(n // tn, m // tm),
        in_specs=[
            pl.BlockSpec((tm, d), lambda j, i: (i, 0)),
            pl.BlockSpec((1, d), lambda j, i: (0, 0)),
            pl.BlockSpec((d, tn), lambda j, i: (0, j)),
        ],
        out_specs=pl.BlockSpec((tm, tn), lambda j, i: (i, j)),
        out_shape=jax.ShapeDtypeStruct((m, n), F32),
        compiler_params=pltpu.CompilerParams(
            dimension_semantics=("arbitrary", "arbitrary"),
            vmem_limit_bytes=40 * 1024 * 1024),
        name="norm_matmul",
    )(x, g.reshape(1, d), w)


def _mixer_kernel(up_ref, xl_ref, gl_ref, x_ref, pool_w_ref, pool_b_ref, pool_s_ref,
                  conv_w_ref, conv_b_ref, ga_w_ref, ga_b_ref, gx_w_ref, gx_b_ref,
                  lam_ref, w_out_ref, o_ref,
                  pool_tail, conv_tail, h_carry, a_scr, b_scr, h_scr):
    ts = up_ref.shape[0]
    pool_width = up_ref.shape[1]
    lru_width = xl_ref.shape[1]
    pool_group = pool_width // len(POOL_WINDOWS)
    head_dim = lru_width // LRU_HEADS
    s = pl.program_id(1)

    @pl.when(s == 0)
    def _():
        pool_tail[...] = jnp.zeros_like(pool_tail)
        conv_tail[...] = jnp.zeros_like(conv_tail)
        h_carry[...] = jnp.zeros_like(h_carry)

    u = up_ref[...]
    ext = jnp.concatenate([pool_tail[...], u], axis=0)
    pool_tail[...] = u[ts - POOL_TAIL:, :]
    pos = (s * ts + 1 + lax.broadcasted_iota(jnp.int32, (ts, 1), 0)).astype(F32)
    ys = []
    for g, w in enumerate(POOL_WINDOWS):
        acc = ext[:, g * pool_group:(g + 1) * pool_group]
        width = 1
        while width < w:
            acc = acc[width:, :] + acc[:-width, :]
            width *= 2
        start = POOL_TAIL + 1 - w
        win = acc[start:start + ts, :]
        d = win / jnp.minimum(pos, float(w)) - u[:, g * pool_group:(g + 1) * pool_group]
        ys.append(jnp.dot(d.astype(BF16), pool_w_ref[g], preferred_element_type=F32))
    y_pool = (jnp.concatenate(ys, axis=1) + pool_b_ref[...]) * pool_s_ref[...]

    xb = xl_ref[...]
    extx = jnp.concatenate([conv_tail[...], xb], axis=0)
    conv_tail[...] = xb[ts - CONV_TAIL:, :]
    xc = jnp.broadcast_to(conv_b_ref[...], xb.shape)
    for k in range(CONV_WIDTH):
        off = CONV_TAIL - (CONV_WIDTH - 1) + k
        xc = xc + extx[off:off + ts, :] * conv_w_ref[k:k + 1, :]
    xcb = xc.astype(BF16)
    rs, is_ = [], []
    for h in range(LRU_HEADS):
        xh = xcb[:, h * head_dim:(h + 1) * head_dim]
        rs.append(jnp.dot(xh, ga_w_ref[h], preferred_element_type=F32))
        is_.append(jnp.dot(xh, gx_w_ref[h], preferred_element_type=F32))
    r = jax.nn.sigmoid(jnp.concatenate(rs, axis=1) + ga_b_ref[...])
    i_gate = jax.nn.sigmoid(jnp.concatenate(is_, axis=1) + gx_b_ref[...])
    lam = lam_ref[...]
    log_sig = jnp.minimum(lam, 0.0) - jnp.log1p(jnp.exp(-jnp.abs(lam)))
    log_a = (LRU_C * r) * log_sig
    a_scr[...] = jnp.exp(log_a)
    th = jnp.tanh(log_a)
    b_scr[...] = jnp.sqrt(-2.0 * th / (1.0 - th)) * (i_gate * xc)

    row = lax.broadcasted_iota(jnp.int32, (SUBLANES, lru_width), 0)

    def scan_group(gi, h_prev):
        r0 = pl.multiple_of(gi * SUBLANES, SUBLANES)
        a = a_scr[pl.ds(r0, SUBLANES), :]
        b = b_scr[pl.ds(r0, SUBLANES), :]
        for d in (1, 2, 4):
            a_sh = jnp.where(row >= d, pltpu.roll(a, d, axis=0), 1.0)
            b_sh = jnp.where(row >= d, pltpu.roll(b, d, axis=0), 0.0)
            b = a * b_sh + b
            a = a * a_sh
        hg = a * h_prev + b
        h_scr[pl.ds(r0, SUBLANES), :] = hg
        return hg[SUBLANES - 1:SUBLANES, :]

    h_last = lax.fori_loop(0, ts // SUBLANES, scan_group, h_carry[0:1, :])
    h_carry[0:1, :] = h_last
    y_lru = h_scr[...] * _gelu_tanh(gl_ref[...])

    y = jnp.concatenate([y_pool, y_lru], axis=1).astype(BF16)
    o_ref[...] = x_ref[...] + jnp.dot(y, w_out_ref[...], preferred_element_type=F32)


def _mixer(proj, x2d, batch, seq, pool_w, pool_b, pool_scale, conv_w, conv_b,
           ga_w, ga_b, gx_w, gx_b, lam, w_out, *, ts=256):
    t, d_model = x2d.shape
    pool_width = pool_b.shape[-1]
    lru_width = lam.shape[-1]
    ns = seq // ts
    row_blk = lambda c: (lambda b, s: (b * ns + s, c))
    full = lambda a: pl.BlockSpec(a.shape, lambda b, s: (0,) * a.ndim)
    consts = [pool_w, pool_b.reshape(1, -1), pool_scale.reshape(1, -1), conv_w,
              conv_b.reshape(1, -1), ga_w, ga_b.reshape(1, -1), gx_w, gx_b.reshape(1, -1),
              lam.reshape(1, -1), w_out]
    return pl.pallas_call(
        _mixer_kernel,
        grid=(batch, ns),
        in_specs=[
            pl.BlockSpec((ts, pool_width), row_blk(0)),
            pl.BlockSpec((ts, lru_width), row_blk(pool_width // lru_width)),
            pl.BlockSpec((ts, lru_width), row_blk(pool_width // lru_width + 1)),
            pl.BlockSpec((ts, d_model), row_blk(0)),
        ] + [full(a) for a in consts],
        out_specs=pl.BlockSpec((ts, d_model), row_blk(0)),
        out_shape=jax.ShapeDtypeStruct((t, d_model), F32),
        scratch_shapes=[
            pltpu.VMEM((POOL_TAIL, pool_width), F32),
            pltpu.VMEM((CONV_TAIL, lru_width), F32),
            pltpu.VMEM((SUBLANES, lru_width), F32),
            pltpu.VMEM((ts, lru_width), F32),
            pltpu.VMEM((ts, lru_width), F32),
            pltpu.VMEM((ts, lru_width), F32),
        ],
        compiler_params=pltpu.CompilerParams(
            dimension_semantics=("arbitrary", "arbitrary"),
            vmem_limit_bytes=48 * 1024 * 1024),
        name="mixer",
    )(proj, proj, proj, x2d, *consts)


def _topk_rows(s, k, payload=None):
    n = s.shape[0]
    row = lax.broadcasted_iota(jnp.int32, s.shape, 0)
    vals, idxs, pays = [], [], []
    for _ in range(k):
        m = jnp.max(s, axis=0, keepdims=True)
        idx = jnp.min(jnp.where(s == m, row, n), axis=0, keepdims=True)
        hit = row == idx
        vals.append(m)
        idxs.append(idx)
        if payload is not None:
            pays.append(jnp.sum(jnp.where(hit, payload, 0), axis=0, keepdims=True))
        s = jnp.where(hit, -jnp.inf, s)
    out = (jnp.concatenate(vals, axis=0), jnp.concatenate(idxs, axis=0))
    if payload is not None:
        out += (jnp.concatenate(pays, axis=0),)
    return out


def _topk_kernel(q_ref, k1_ref, k2_ref, exp_ref, gate_ref):
    half = k1_ref.shape[2]
    nt = (((1,), (1,)), ((), ()))

    def head(h, carry):
        c0 = pl.multiple_of(h * 2 * half, 2 * half)
        q1 = q_ref[:, pl.ds(c0, half)].astype(BF16)
        q2 = q_ref[:, pl.ds(c0 + half, half)].astype(BF16)
        s1 = lax.dot_general(k1_ref[h], q1, nt, preferred_element_type=F32)
        s2 = lax.dot_general(k2_ref[h], q2, nt, preferred_element_type=F32)
        v1, i1 = _topk_rows(s1, PEER_TOPK)
        v2, i2 = _topk_rows(s2, PEER_TOPK)
        cand = jnp.concatenate([v1[a:a + 1, :] + v2 for a in range(PEER_TOPK)], axis=0)
        cidx = jnp.concatenate([i1[a:a + 1, :] * PEER_N_KEYS + i2 for a in range(PEER_TOPK)], axis=0)
        top_s, _, expert = _topk_rows(cand, PEER_TOPK, payload=cidx)
        e = jnp.exp(top_s - jnp.max(top_s, axis=0, keepdims=True))
        gate = e / jnp.sum(e, axis=0, keepdims=True)
        r0 = pl.multiple_of(h * PEER_TOPK, PEER_TOPK)
        exp_ref[pl.ds(r0, PEER_TOPK), :] = expert
        gate_ref[pl.ds(r0, PEER_TOPK), :] = gate
        return carry

    lax.fori_loop(0, PEER_HEADS, head, 0)


def _topk(q, k1, k2, *, tb=256):
    t, dq = q.shape
    npairs = PEER_HEADS * PEER_TOPK
    return pl.pallas_call(
        _topk_kernel,
        grid=(t // tb,),
        in_specs=[
            pl.BlockSpec((tb, dq), lambda i: (i, 0)),
            pl.BlockSpec(k1.shape, lambda i: (0, 0, 0)),
            pl.BlockSpec(k2.shape, lambda i: (0, 0, 0)),
        ],
        out_specs=[
            pl.BlockSpec((npairs, tb), lambda i: (0, i)),
            pl.BlockSpec((npairs, tb), lambda i: (0, i)),
        ],
        out_shape=[
            jax.ShapeDtypeStruct((npairs, t), jnp.int32),
            jax.ShapeDtypeStruct((npairs, t), F32),
        ],
        compiler_params=pltpu.CompilerParams(dimension_semantics=("arbitrary",)),
        name="peer_topk",
    )(q, k1, k2)


PEER_SLOTS = 4
PEER_LOOKAHEAD = 8


def _peer_kernel(idx_hbm, gate_ref, h_ref, g2_ref, gf_ref, tab_hbm, o_ref,
                 idx_smem, idx_sems, buf, sems, z_scr, p_scr):
    tbp = h_ref.shape[0]
    nchunk = buf.shape[1]
    npairs = buf.shape[2]
    ngroups = npairs // SUBLANES
    blk = tbp * npairs
    win = (tbp + PEER_LOOKAHEAD) * npairs
    ahead = PEER_SLOTS - 1
    step = pl.program_id(0)
    nsteps = pl.num_programs(0)
    base = lax.rem(step, 2) * win

    def idx_copy(s):
        half = lax.rem(s, 2)
        return pltpu.make_async_copy(
            idx_hbm.at[pl.ds(pl.multiple_of(s * blk, blk), win)],
            idx_smem.at[pl.ds(pl.multiple_of(half * win, win), win)],
            idx_sems.at[half])

    def issue(t, slot):
        for k in range(npairs):
            e = idx_smem[base + t * npairs + k]
            pltpu.make_async_copy(tab_hbm.at[e], buf.at[slot, :, k, :], sems.at[slot]).start()

    def wait(slot):
        pltpu.make_async_copy(buf.at[slot], buf.at[slot], sems.at[slot]).wait()

    @pl.when(step == 0)
    def _():
        idx_copy(step).start()

    idx_copy(step).wait()

    @pl.when(step + 1 < nsteps)
    def _():
        idx_copy(step + 1).start()

    @pl.when(step == 0)
    def _():
        for t0 in range(ahead):
            issue(t0, t0)

    h1 = h_ref[...]
    z_scr[...] = _rmsnorm(h1, g2_ref[...])
    hi_mask = jnp.uint32(0xFFFF0000)

    def compute(t, slot):
        zrow = z_scr[pl.ds(t, 1), :]
        zb = [jnp.broadcast_to(zrow[:, j * LANES:(j + 1) * LANES], (SUBLANES, LANES)) for j in range(nchunk)]
        gate_t = gate_ref[t]
        oacc = [jnp.zeros((SUBLANES, LANES), F32) for _ in range(nchunk)]
        for pg in range(ngroups):
            acc = jnp.zeros((SUBLANES, LANES), F32)
            for j in range(nchunk):
                xw = buf[slot, j, pg * SUBLANES:(pg + 1) * SUBLANES, :]
                acc = acc + pltpu.bitcast(xw & hi_mask, F32) * zb[j]
            act = jnp.sum(acc, axis=1, keepdims=True)
            w = gate_t[:, pg:pg + 1] * _gelu_tanh(act)
            wb = jnp.broadcast_to(w, (SUBLANES, LANES))
            for j in range(nchunk):
                xw = buf[slot, j, pg * SUBLANES:(pg + 1) * SUBLANES, :]
                oacc[j] = oacc[j] + pltpu.bitcast(xw << 16, F32) * wb
        out_row = jnp.concatenate([jnp.sum(o, axis=0, keepdims=True) for o in oacc], axis=1)
        p_scr[pl.ds(t, 1), :] = out_row

    def group(gi, carry):
        for s in range(PEER_SLOTS):
            t = gi * PEER_SLOTS + s
            wait(s)
            issue(t + ahead, (s + ahead) % PEER_SLOTS)
            compute(t, s)
        return carry

    lax.fori_loop(0, tbp // PEER_SLOTS, group, 0)

    @pl.when(step == nsteps - 1)
    def _():
        for t0 in range(ahead):
            wait(t0)

    o_ref[...] = _rmsnorm(h1 + p_scr[...], gf_ref[...])


def _peer(idx_flat, gate3, h1, g2, gf, table, *, tbp=128):
    t, d = h1.shape
    npairs = PEER_HEADS * PEER_TOPK
    nchunk = table.shape[1]
    return pl.pallas_call(
        _peer_kernel,
        grid=(t // tbp,),
        in_specs=[
            pl.BlockSpec(memory_space=pl.ANY),
            pl.BlockSpec((tbp,) + gate3.shape[1:], lambda i: (i, 0, 0)),
            pl.BlockSpec((tbp, d), lambda i: (i, 0)),
            pl.BlockSpec((1, d), lambda i: (0, 0)),
            pl.BlockSpec((1, d), lambda i: (0, 0)),
            pl.BlockSpec(memory_space=pl.ANY),
        ],
        out_specs=pl.BlockSpec((tbp, d), lambda i: (i, 0)),
        out_shape=jax.ShapeDtypeStruct((t, d), F32),
        scratch_shapes=[
            pltpu.SMEM((2 * (tbp + PEER_LOOKAHEAD) * npairs,), jnp.int32),
            pltpu.SemaphoreType.DMA((2,)),
            pltpu.VMEM((PEER_SLOTS, nchunk, npairs, LANES), jnp.uint32),
            pltpu.SemaphoreType.DMA((PEER_SLOTS,)),
            pltpu.VMEM((tbp, d), F32),
            pltpu.VMEM((tbp, d), F32),
        ],
        compiler_params=pltpu.CompilerParams(
            dimension_semantics=("arbitrary",),
            vmem_limit_bytes=40 * 1024 * 1024),
        name="peer_experts",
    )(idx_flat, gate3, h1, g2.reshape(1, d), gf.reshape(1, d), table)


def _pack_expert_table(peer_u, peer_v):
    n, d = peer_u.shape
    ub = lax.bitcast_convert_type(peer_u.astype(BF16), jnp.uint16).astype(jnp.uint32)
    vb = lax.bitcast_convert_type(peer_v.astype(BF16), jnp.uint16).astype(jnp.uint32)
    return ((ub << 16) | vb).reshape(n, d // LANES, LANES)


def kernel(x, norm1_g, w_in, pool_w, pool_b, pool_scale, conv_w, conv_b, gate_a_w, gate_a_b,
           gate_x_w, gate_x_b, lru_lambda, w_out, norm2_g, peer_wq, peer_keys1, peer_keys2,
           peer_u, peer_v, norm_f_g):
    batch, seq, d_model = x.shape
    assert w_in.shape[0] == 1, "the final norm is fused into the single layer's PEER kernel"
    npairs = PEER_HEADS * PEER_TOPK
    h = x.reshape(batch * seq, d_model)
    for l in range(1):
        proj = _norm_matmul(h, norm1_g[l], w_in[l].astype(BF16))
        h = _mixer(proj, h, batch, seq, pool_w[l].astype(BF16), pool_b[l].reshape(-1), pool_scale[l],
                   conv_w[l], conv_b[l], gate_a_w[l].astype(BF16), gate_a_b[l].reshape(-1),
                   gate_x_w[l].astype(BF16), gate_x_b[l].reshape(-1), lru_lambda[l],
                   w_out[l].astype(BF16))
        q = _norm_matmul(h, norm2_g[l], peer_wq[l].astype(BF16))
        expert, gate = _topk(q, peer_keys1[l].astype(BF16), peer_keys2[l].astype(BF16))
        idx_flat = jnp.pad(expert.T.reshape(-1), (0, PEER_LOOKAHEAD * npairs))
        gate3 = gate.T.reshape(-1, npairs // SUBLANES, SUBLANES).transpose(0, 2, 1)
        table = _pack_expert_table(peer_u[l], peer_v[l])
        h = _peer(idx_flat, gate3, h, norm2_g[l], norm_f_g, table)
    return h.reshape(batch, seq, d_model)
```

```python
import functools

import jax
import jax.numpy as jnp
from jax import lax
from jax.experimental import pallas as pl
from jax.experimental.pallas import tpu as pltpu

EPS = 1e-6
POOL_WINDOWS = (2, 4, 8, 16)
POOL_TAIL = 16
CONV_WIDTH = 4
CONV_TAIL = 8
LRU_C = 8.0
LRU_HEADS = 8
PEER_HEADS = 8
PEER_TOPK = 16
PEER_N_KEYS = 128

LANES = 128
SUBLANES = 8

F32 = jnp.float32
BF16 = jnp.bfloat16


def _rmsnorm(x, g):
    return x * lax.rsqrt(jnp.mean(x * x, axis=-1, keepdims=True) + EPS) * g


def _gelu_tanh(x):
    c = 0.7978845608028654
    return x * (0.5 * (1.0 + jnp.tanh(c * (x + 0.044715 * (x * x * x)))))


def _norm_matmul_kernel(x_ref, g_ref, w_ref, o_ref):
    z = _rmsnorm(x_ref[...], g_ref[...])
    o_ref[...] = jnp.dot(z.astype(BF16), w_ref[...], preferred_element_type=F32)


def _norm_matmul(x, g, w, *, tm=512, tn=1024):
    m, d = x.shape
    n = w.shape[1]
    return pl.pallas_call(
        _norm_matmul_kernel,
        grid=(n // tn, m // tm),
        in_specs=[
            pl.BlockSpec((tm, d), lambda j, i: (i, 0)),
            pl.BlockSpec((1, d), lambda j, i: (0, 0)),
            pl.BlockSpec((d, tn), lambda j, i: (0, j)),
        ],
        out_specs=pl.BlockSpec((tm, tn), lambda j, i: (i, j)),
        out_shape=jax.ShapeDtypeStruct((m, n), F32),
        compiler_params=pltpu.CompilerParams(
            dimension_semantics=("arbitrary", "arbitrary"),
            vmem_limit_bytes=40 * 1024 * 1024),
        name="norm_matmul",
    )(x, g.reshape(1, d), w)


def _mixer_kernel(up_ref, xl_ref, gl_ref, x_ref, pool_w_ref, pool_b_ref, pool_s_ref,
                  conv_w_ref, conv_b_ref, ga_w_ref, ga_b_ref, gx_w_ref, gx_b_ref,
                  lam_ref, w_out_ref, o_ref,
                  pool_tail, conv_tail, h_carry, a_scr, b_scr, h_scr):
    ts = up_ref.shape[0]
    pool_width = up_ref.shape[1]
    lru_width = xl_ref.shape[1]
    pool_group = pool_width // len(POOL_WINDOWS)
    head_dim = lru_width // LRU_HEADS
    s = pl.program_id(1)

    @pl.when(s == 0)
    def _():
        pool_tail[...] = jnp.zeros_like(pool_tail)
        conv_tail[...] = jnp.zeros_like(conv_tail)
        h_carry[...] = jnp.zeros_like(h_carry)

    u = up_ref[...]
    ext = jnp.concatenate([pool_tail[...], u], axis=0)
    pool_tail[...] = u[ts - POOL_TAIL:, :]
    pos = (s * ts + 1 + lax.broadcasted_iota(jnp.int32, (ts, 1), 0)).astype(F32)
    ys = []
    for g, w in enumerate(POOL_WINDOWS):
        acc = ext[:, g * pool_group:(g + 1) * pool_group]
        width = 1
        while width < w:
            acc = acc[width:, :] + acc[:-width, :]
            width *= 2
        start = POOL_TAIL + 1 - w
        win = acc[start:start + ts, :]
        d = win / jnp.minimum(pos, float(w)) - u[:, g * pool_group:(g + 1) * pool_group]
        ys.append(jnp.dot(d.astype(BF16), pool_w_ref[g], preferred_element_type=F32))
    y_pool = (jnp.concatenate(ys, axis=1) + pool_b_ref[...]) * pool_s_ref[...]

    xb = xl_ref[...]
    extx = jnp.concatenate([conv_tail[...], xb], axis=0)
    conv_tail[...] = xb[ts - CONV_TAIL:, :]
    xc = jnp.broadcast_to(conv_b_ref[...], xb.shape)
    for k in range(CONV_WIDTH):
        off = CONV_TAIL - (CONV_WIDTH - 1) + k
        xc = xc + extx[off:off + ts, :] * conv_w_ref[k:k + 1, :]
    xcb = xc.astype(BF16)
    rs, is_ = [], []
    for h in range(LRU_HEADS):
        xh = xcb[:, h * head_dim:(h + 1) * head_dim]
        rs.append(jnp.dot(xh, ga_w_ref[h], preferred_element_type=F32))
        is_.append(jnp.dot(xh, gx_w_ref[h], preferred_element_type=F32))
    r = jax.nn.sigmoid(jnp.concatenate(rs, axis=1) + ga_b_ref[...])
    i_gate = jax.nn.sigmoid(jnp.concatenate(is_, axis=1) + gx_b_ref[...])
    lam = lam_ref[...]
    log_sig = jnp.minimum(lam, 0.0) - jnp.log1p(jnp.exp(-jnp.abs(lam)))
    log_a = (LRU_C * r) * log_sig
    a_scr[...] = jnp.exp(log_a)
    th = jnp.tanh(log_a)
    b_scr[...] = jnp.sqrt(-2.0 * th / (1.0 - th)) * (i_gate * xc)

    row = lax.broadcasted_iota(jnp.int32, (SUBLANES, lru_width), 0)

    def scan_group(gi, h_prev):
        r0 = pl.multiple_of(gi * SUBLANES, SUBLANES)
        a = a_scr[pl.ds(r0, SUBLANES), :]
        b = b_scr[pl.ds(r0, SUBLANES), :]
        for d in (1, 2, 4):
            a_sh = jnp.where(row >= d, pltpu.roll(a, d, axis=0), 1.0)
            b_sh = jnp.where(row >= d, pltpu.roll(b, d, axis=0), 0.0)
            b = a * b_sh + b
            a = a * a_sh
        hg = a * h_prev + b
        h_scr[pl.ds(r0, SUBLANES), :] = hg
        return hg[SUBLANES - 1:SUBLANES, :]

    h_last = lax.fori_loop(0, ts // SUBLANES, scan_group, h_carry[0:1, :])
    h_carry[0:1, :] = h_last
    y_lru = h_scr[...] * _gelu_tanh(gl_ref[...])

    y = jnp.concatenate([y_pool, y_lru], axis=1).astype(BF16)
    o_ref[...] = x_ref[...] + jnp.dot(y, w_out_ref[...], preferred_element_type=F32)


def _mixer(proj, x2d, batch, seq, pool_w, pool_b, pool_scale, conv_w, conv_b,
           ga_w, ga_b, gx_w, gx_b, lam, w_out, *, ts=256):
    t, d_model = x2d.shape
    pool_width = pool_b.shape[-1]
    lru_width = lam.shape[-1]
    ns = seq // ts
    row_blk = lambda c: (lambda b, s: (b * ns + s, c))
    full = lambda a: pl.BlockSpec(a.shape, lambda b, s: (0,) * a.ndim)
    consts = [pool_w, pool_b.reshape(1, -1), pool_scale.reshape(1, -1), conv_w,
              conv_b.reshape(1, -1), ga_w, ga_b.reshape(1, -1), gx_w, gx_b.reshape(1, -1),
              lam.reshape(1, -1), w_out]
    return pl.pallas_call(
        _mixer_kernel,
        grid=(batch, ns),
        in_specs=[
            pl.BlockSpec((ts, pool_width), row_blk(0)),
            pl.BlockSpec((ts, lru_width), row_blk(pool_width // lru_width)),
            pl.BlockSpec((ts, lru_width), row_blk(pool_width // lru_width + 1)),
            pl.BlockSpec((ts, d_model), row_blk(0)),
        ] + [full(a) for a in consts],
        out_specs=pl.BlockSpec((ts, d_model), row_blk(0)),
        out_shape=jax.ShapeDtypeStruct((t, d_model), F32),
        scratch_shapes=[
            pltpu.VMEM((POOL_TAIL, pool_width), F32),
            pltpu.VMEM((CONV_TAIL, lru_width), F32),
            pltpu.VMEM((SUBLANES, lru_width), F32),
            pltpu.VMEM((ts, lru_width), F32),
            pltpu.VMEM((ts, lru_width), F32),
            pltpu.VMEM((ts, lru_width), F32),
        ],
        compiler_params=pltpu.CompilerParams(
            dimension_semantics=("arbitrary", "arbitrary"),
            vmem_limit_bytes=48 * 1024 * 1024),
        name="mixer",
    )(proj, proj, proj, x2d, *consts)


def _topk_rows(s, k, payload=None):
    n = s.shape[0]
    row = lax.broadcasted_iota(jnp.int32, s.shape, 0)
    vals, idxs, pays = [], [], []
    for _ in range(k):
        m = jnp.max(s, axis=0, keepdims=True)
        idx = jnp.min(jnp.where(s == m, row, n), axis=0, keepdims=True)
        hit = row == idx
        vals.append(m)
        idxs.append(idx)
        if payload is not None:
            pays.append(jnp.sum(jnp.where(hit, payload, 0), axis=0, keepdims=True))
        s = jnp.where(hit, -jnp.inf, s)
    out = (jnp.concatenate(vals, axis=0), jnp.concatenate(idxs, axis=0))
    if payload is not None:
        out += (jnp.concatenate(pays, axis=0),)
    return out


def _topk_kernel(q_ref, k1_ref, k2_ref, exp_ref, gate_ref):
    half = k1_ref.shape[2]
    nt = (((1,), (1,)), ((), ()))

    def head(h, carry):
        c0 = pl.multiple_of(h * 2 * half, 2 * half)
        q1 = q_ref[:, pl.ds(c0, half)].astype(BF16)
        q2 = q_ref[:, pl.ds(c0 + half, half)].astype(BF16)
        s1 = lax.dot_general(k1_ref[h], q1, nt, preferred_element_type=F32)
        s2 = lax.dot_general(k2_ref[h], q2, nt, preferred_element_type=F32)
        v1, i1 = _topk_rows(s1, PEER_TOPK)
        v2, i2 = _topk_rows(s2, PEER_TOPK)
        cand = jnp.concatenate([v1[a:a + 1, :] + v2 for a in range(PEER_TOPK)], axis=0)
        cidx = jnp.concatenate([i1[a:a + 1, :] * PEER_N_KEYS + i2 for a in range(PEER_TOPK)], axis=0)
        top_s, _, expert = _topk_rows(cand, PEER_TOPK, payload=cidx)
        e = jnp.exp(top_s - jnp.max(top_s, axis=0, keepdims=True))
        gate = e / jnp.sum(e, axis=0, keepdims=True)
        r0 = pl.multiple_of(h * PEER_TOPK, PEER_TOPK)
        exp_ref[pl.ds(r0, PEER_TOPK), :] = expert
        gate_ref[pl.ds(r0, PEER_TOPK), :] = gate
        return carry

    lax.fori_loop(0, PEER_HEADS, head, 0)


def _topk(q, k1, k2, *, tb=256):
    t, dq = q.shape
    npairs = PEER_HEADS * PEER_TOPK
    return pl.pallas_call(
        _topk_kernel,
        grid=(t // tb,),
        in_specs=[
            pl.BlockSpec((tb, dq), lambda i: (i, 0)),
            pl.BlockSpec(k1.shape, lambda i: (0, 0, 0)),
            pl.BlockSpec(k2.shape, lambda i: (0, 0, 0)),
        ],
        out_specs=[
            pl.BlockSpec((npairs, tb), lambda i: (0, i)),
            pl.BlockSpec((npairs, tb), lambda i: (0, i)),
        ],
        out_shape=[
            jax.ShapeDtypeStruct((npairs, t), jnp.int32),
            jax.ShapeDtypeStruct((npairs, t), F32),
        ],
        compiler_params=pltpu.CompilerParams(dimension_semantics=("arbitrary",)),
        name="peer_topk",
    )(q, k1, k2)


PEER_GROUP = 4
PEER_PHASES = 4
PEER_SLOTS = PEER_GROUP * PEER_PHASES
PEER_LOOKAHEAD = PEER_GROUP * PEER_PHASES


def _peer_kernel(idx_hbm, gate_ref, h_ref, g2_ref, gf_ref, tab_hbm, o_ref,
                 idx_smem, idx_sems, *scratch):
    tbp = h_ref.shape[0]
    bufs = scratch[:PEER_PHASES]
    sems, z_scr, p_scr = scratch[PEER_PHASES:]
    nchunk = bufs[0].shape[1]
    npairs = bufs[0].shape[2]
    nhalf = nchunk // 2
    blk = tbp * npairs
    win = (tbp + PEER_LOOKAHEAD) * npairs
    ahead = PEER_PHASES - 1
    ngroups = tbp // PEER_GROUP
    assert ngroups % PEER_PHASES == 0
    step = pl.program_id(0)
    nsteps = pl.num_programs(0)
    base = lax.rem(step, 2) * win

    def idx_copy(s):
        half = lax.rem(s, 2)
        return pltpu.make_async_copy(
            idx_hbm.at[pl.ds(pl.multiple_of(s * blk, blk), win)],
            idx_smem.at[pl.ds(pl.multiple_of(half * win, win), win)],
            idx_sems.at[half])

    def issue(t, phase, i):
        for k in range(npairs):
            e = idx_smem[base + t * npairs + k]
            pltpu.make_async_copy(tab_hbm.at[e], bufs[phase].at[i, :, k, :],
                                  sems.at[phase * PEER_GROUP + i]).start()

    def wait(phase, i):
        pltpu.make_async_copy(bufs[phase].at[i], bufs[phase].at[i], sems.at[phase * PEER_GROUP + i]).wait()

    def issue_group(g, phase):
        for i in range(PEER_GROUP):
            issue(g * PEER_GROUP + i, phase, i)

    def wait_group(phase):
        for i in range(PEER_GROUP):
            wait(phase, i)

    @pl.when(step == 0)
    def _():
        idx_copy(step).start()

    idx_copy(step).wait()

    @pl.when(step + 1 < nsteps)
    def _():
        idx_copy(step + 1).start()

    @pl.when(step == 0)
    def _():
        for g0 in range(ahead):
            issue_group(g0, g0)

    h1 = h_ref[...]
    z_scr[...] = _rmsnorm(h1, g2_ref[...])
    half_w = nhalf * LANES
    lane2 = lax.broadcasted_iota(jnp.int32, (1, 2 * npairs), 1)
    even = (lane2 & 1) == 0
    pad_rows = jnp.zeros((SUBLANES - 4, 2 * npairs), BF16)
    nt = (((1,), (1,)), ((), ()))

    def split(x):
        hi = x.astype(BF16)
        return hi, (x - hi.astype(F32)).astype(BF16)

    def neuron_inputs(t, rows):
        zh, zl = split(z_scr[pl.ds(t, 1), :])
        zpad = jnp.zeros((SUBLANES - 4, half_w), BF16)
        lhs = jnp.concatenate([zh[:, :half_w], zh[:, half_w:], zl[:, :half_w], zl[:, half_w:], zpad], axis=0)
        uw = jnp.concatenate([pltpu.bitcast(rows[j], BF16) for j in range(nhalf)], axis=1)
        return lax.dot_general(lhs, uw, nt, preferred_element_type=F32)

    def neuron_weights(t, acc):
        part = jnp.where(even, acc[0:1] + acc[2:3], acc[1:2] + acc[3:4])
        other = jnp.where(even, pltpu.roll(part, 2 * npairs - 1, axis=1), pltpu.roll(part, 1, axis=1))
        w = gate_ref[pl.ds(t, 1), :] * _gelu_tanh(part + other)
        wh, wl = split(w)
        zero = jnp.zeros_like(wh)
        return jnp.concatenate([jnp.where(even, wh, zero), jnp.where(even, zero, wh),
                                jnp.where(even, wl, zero), jnp.where(even, zero, wl), pad_rows], axis=0)

    def neuron_outputs(t, rows, lhs):
        vw = jnp.concatenate([pltpu.bitcast(rows[nhalf + j], BF16) for j in range(nhalf)], axis=1)
        r = jnp.dot(lhs, vw, preferred_element_type=F32)
        p_scr[pl.ds(t, 1), :] = jnp.concatenate([r[0:1] + r[2:3], r[1:2] + r[3:4]], axis=1)

    def round_(r, carry):
        for phase in range(PEER_PHASES):
            g = r * PEER_PHASES + phase
            wait_group(phase)
            toks = [(g * PEER_GROUP + i, bufs[phase].at[i]) for i in range(PEER_GROUP)]
            nxt = (phase + ahead) % PEER_PHASES
            accs = []
            for i, (t, rows) in enumerate(toks):
                issue((g + ahead) * PEER_GROUP + i, nxt, i)
                accs.append(neuron_inputs(t, rows))
            lhss = [neuron_weights(t, acc) for (t, _), acc in zip(toks, accs)]
            for (t, rows), lhs in zip(toks, lhss):
                neuron_outputs(t, rows, lhs)
        return carry

    lax.fori_loop(0, ngroups // PEER_PHASES, round_, 0)

    @pl.when(step == nsteps - 1)
    def _():
        for g0 in range(ahead):
            wait_group(g0)

    o_ref[...] = _rmsnorm(h1 + p_scr[...], gf_ref[...])


def _peer(idx_flat, gate2, h1, g2, gf, table, *, tbp=128):
    t, d = h1.shape
    npairs = PEER_HEADS * PEER_TOPK
    nchunk = table.shape[1]
    return pl.pallas_call(
        _peer_kernel,
        grid=(t // tbp,),
        in_specs=[
            pl.BlockSpec(memory_space=pl.ANY),
            pl.BlockSpec((tbp, 2 * npairs), lambda i: (i, 0)),
            pl.BlockSpec((tbp, d), lambda i: (i, 0)),
            pl.BlockSpec((1, d), lambda i: (0, 0)),
            pl.BlockSpec((1, d), lambda i: (0, 0)),
            pl.BlockSpec(memory_space=pl.ANY),
        ],
        out_specs=pl.BlockSpec((tbp, d), lambda i: (i, 0)),
        out_shape=jax.ShapeDtypeStruct((t, d), F32),
        scratch_shapes=[
            pltpu.SMEM((2 * (tbp + PEER_LOOKAHEAD) * npairs,), jnp.int32),
            pltpu.SemaphoreType.DMA((2,)),
        ] + [pltpu.VMEM((PEER_GROUP, nchunk, npairs, LANES), jnp.uint32)] * PEER_PHASES + [
            pltpu.SemaphoreType.DMA((PEER_SLOTS,)),
            pltpu.VMEM((tbp, d), F32),
            pltpu.VMEM((tbp, d), F32),
        ],
        compiler_params=pltpu.CompilerParams(
            dimension_semantics=("arbitrary",),
            vmem_limit_bytes=40 * 1024 * 1024),
        name="peer_experts",
    )(idx_flat, gate2, h1, g2.reshape(1, d), gf.reshape(1, d), table)


def _pack_expert_table(peer_u, peer_v):
    n, d = peer_u.shape

    def words(w):
        b = lax.bitcast_convert_type(w.astype(BF16), jnp.uint16).astype(jnp.uint32)
        return b[:, :d // 2] | (b[:, d // 2:] << 16)

    return jnp.concatenate([words(peer_u), words(peer_v)], axis=1).reshape(n, d // LANES, LANES)


def kernel(x, norm1_g, w_in, pool_w, pool_b, pool_scale, conv_w, conv_b, gate_a_w, gate_a_b,
           gate_x_w, gate_x_b, lru_lambda, w_out, norm2_g, peer_wq, peer_keys1, peer_keys2,
           peer_u, peer_v, norm_f_g):
    batch, seq, d_model = x.shape
    assert w_in.shape[0] == 1, "the final norm is fused into the single layer's PEER kernel"
    npairs = PEER_HEADS * PEER_TOPK
    h = x.reshape(batch * seq, d_model)
    for l in range(1):
        proj = _norm_matmul(h, norm1_g[l], w_in[l].astype(BF16))
        h = _mixer(proj, h, batch, seq, pool_w[l].astype(BF16), pool_b[l].reshape(-1), pool_scale[l],
                   conv_w[l], conv_b[l], gate_a_w[l].astype(BF16), gate_a_b[l].reshape(-1),
                   gate_x_w[l].astype(BF16), gate_x_b[l].reshape(-1), lru_lambda[l],
                   w_out[l].astype(BF16))
        q = _norm_matmul(h, norm2_g[l], peer_wq[l].astype(BF16))
        expert, gate = _topk(q, peer_keys1[l].astype(BF16), peer_keys2[l].astype(BF16))
        idx_flat = jnp.pad(expert.T.reshape(-1), (0, PEER_LOOKAHEAD * npairs))
        gate2 = jnp.repeat(gate.T, 2, axis=1)
        table = _pack_expert_table(peer_u[l], peer_v[l])
        h = _peer(idx_flat, gate2, h, norm2_g[l], norm_f_g, table)
    return h.reshape(batch, seq, d_model)
```

```python
import functools

import jax
import jax.numpy as jnp
from jax import lax
from jax.experimental import pallas as pl
from jax.experimental.pallas import tpu as pltpu

EPS = 1e-6
POOL_WINDOWS = (2, 4, 8, 16)
POOL_TAIL = 16
CONV_WIDTH = 4
CONV_TAIL = 8
LRU_C = 8.0
LRU_HEADS = 8
PEER_HEADS = 8
PEER_TOPK = 16
PEER_N_KEYS = 128

LANES = 128
SUBLANES = 8

F32 = jnp.float32
BF16 = jnp.bfloat16


def _rmsnorm(x, g):
    return x * lax.rsqrt(jnp.mean(x * x, axis=-1, keepdims=True) + EPS) * g


def _gelu_tanh(x):
    c = 0.7978845608028654
    return x * (0.5 * (1.0 + jnp.tanh(c * (x + 0.044715 * (x * x * x)))))


def _norm_matmul_kernel(x_ref, g_ref, w_ref, o_ref):
    z = _rmsnorm(x_ref[...], g_ref[...])
    o_ref[...] = jnp.dot(z.astype(BF16), w_ref[...], preferred_element_type=F32)


def _norm_matmul(x, g, w, *, tm=512, tn=1024):
    m, d = x.shape
    n = w.shape[1]
    return pl.pallas_call(
        _norm_matmul_kernel,
        grid=(n // tn, m // tm),
        in_specs=[
            pl.BlockSpec((tm, d), lambda j, i: (i, 0)),
            pl.BlockSpec((1, d), lambda j, i: (0, 0)),
            pl.BlockSpec((d, tn), lambda j, i: (0, j)),
        ],
        out_specs=pl.BlockSpec((tm, tn), lambda j, i: (i, j)),
        out_shape=jax.ShapeDtypeStruct((m, n), F32),
        compiler_params=pltpu.CompilerParams(
            dimension_semantics=("arbitrary", "arbitrary"),
            vmem_limit_bytes=40 * 1024 * 1024),
        name="norm_matmul",
    )(x, g.reshape(1, d), w)


def _mixer_kernel(up_ref, xl_ref, gl_ref, x_ref, pool_w_ref, pool_b_ref, pool_s_ref,
                  conv_w_ref, conv_b_ref, ga_w_ref, ga_b_ref, gx_w_ref, gx_b_ref,
                  lam_ref, w_out_ref, o_ref,
                  pool_tail, conv_tail, h_carry, a_scr, b_scr, h_scr):
    ts = up_ref.shape[0]
    pool_width = up_ref.shape[1]
    lru_width = xl_ref.shape[1]
    pool_group = pool_width // len(POOL_WINDOWS)
    head_dim = lru_width // LRU_HEADS
    s = pl.program_id(1)

    @pl.when(s == 0)
    def _():
        pool_tail[...] = jnp.zeros_like(pool_tail)
        conv_tail[...] = jnp.zeros_like(conv_tail)
        h_carry[...] = jnp.zeros_like(h_carry)

    u = up_ref[...]
    ext = jnp.concatenate([pool_tail[...], u], axis=0)
    pool_tail[...] = u[ts - POOL_TAIL:, :]
    pos = (s * ts + 1 + lax.broadcasted_iota(jnp.int32, (ts, 1), 0)).astype(F32)
    ys = []
    for g, w in enumerate(POOL_WINDOWS):
        acc = ext[:, g * pool_group:(g + 1) * pool_group]
        width = 1
        while width < w:
            acc = acc[width:, :] + acc[:-width, :]
            width *= 2
        start = POOL_TAIL + 1 - w
        win = acc[start:start + ts, :]
        d = win / jnp.minimum(pos, float(w)) - u[:, g * pool_group:(g + 1) * pool_group]
        ys.append(jnp.dot(d.astype(BF16), pool_w_ref[g], preferred_element_type=F32))
    y_pool = (jnp.concatenate(ys, axis=1) + pool_b_ref[...]) * pool_s_ref[...]

    xb = xl_ref[...]
    extx = jnp.concatenate([conv_tail[...], xb], axis=0)
    conv_tail[...] = xb[ts - CONV_TAIL:, :]
    xc = jnp.broadcast_to(conv_b_ref[...], xb.shape)
    for k in range(CONV_WIDTH):
        off = CONV_TAIL - (CONV_WIDTH - 1) + k
        xc = xc + extx[off:off + ts, :] * conv_w_ref[k:k + 1, :]
    xcb = xc.astype(BF16)
    rs, is_ = [], []
    for h in range(LRU_HEADS):
        xh = xcb[:, h * head_dim:(h + 1) * head_dim]
        rs.append(jnp.dot(xh, ga_w_ref[h], preferred_element_type=F32))
        is_.append(jnp.dot(xh, gx_w_ref[h], preferred_element_type=F32))
    r = jax.nn.sigmoid(jnp.concatenate(rs, axis=1) + ga_b_ref[...])
    i_gate = jax.nn.sigmoid(jnp.concatenate(is_, axis=1) + gx_b_ref[...])
    lam = lam_ref[...]
    log_sig = jnp.minimum(lam, 0.0) - jnp.log1p(jnp.exp(-jnp.abs(lam)))
    log_a = (LRU_C * r) * log_sig
    a_scr[...] = jnp.exp(log_a)
    th = jnp.tanh(log_a)
    b_scr[...] = jnp.sqrt(-2.0 * th / (1.0 - th)) * (i_gate * xc)

    row = lax.broadcasted_iota(jnp.int32, (SUBLANES, lru_width), 0)

    def scan_group(gi, h_prev):
        r0 = pl.multiple_of(gi * SUBLANES, SUBLANES)
        a = a_scr[pl.ds(r0, SUBLANES), :]
        b = b_scr[pl.ds(r0, SUBLANES), :]
        for d in (1, 2, 4):
            a_sh = jnp.where(row >= d, pltpu.roll(a, d, axis=0), 1.0)
            b_sh = jnp.where(row >= d, pltpu.roll(b, d, axis=0), 0.0)
            b = a * b_sh + b
            a = a * a_sh
        hg = a * h_prev + b
        h_scr[pl.ds(r0, SUBLANES), :] = hg
        return hg[SUBLANES - 1:SUBLANES, :]

    h_last = lax.fori_loop(0, ts // SUBLANES, scan_group, h_carry[0:1, :])
    h_carry[0:1, :] = h_last
    y_lru = h_scr[...] * _gelu_tanh(gl_ref[...])

    y = jnp.concatenate([y_pool, y_lru], axis=1).astype(BF16)
    o_ref[...] = x_ref[...] + jnp.dot(y, w_out_ref[...], preferred_element_type=F32)


def _mixer(proj, x2d, batch, seq, pool_w, pool_b, pool_scale, conv_w, conv_b,
           ga_w, ga_b, gx_w, gx_b, lam, w_out, *, ts=256):
    t, d_model = x2d.shape
    pool_width = pool_b.shape[-1]
    lru_width = lam.shape[-1]
    ns = seq // ts
    row_blk = lambda c: (lambda b, s: (b * ns + s, c))
    full = lambda a: pl.BlockSpec(a.shape, lambda b, s: (0,) * a.ndim)
    consts = [pool_w, pool_b.reshape(1, -1), pool_scale.reshape(1, -1), conv_w,
              conv_b.reshape(1, -1), ga_w, ga_b.reshape(1, -1), gx_w, gx_b.reshape(1, -1),
              lam.reshape(1, -1), w_out]
    return pl.pallas_call(
        _mixer_kernel,
        grid=(batch, ns),
        in_specs=[
            pl.BlockSpec((ts, pool_width), row_blk(0)),
            pl.BlockSpec((ts, lru_width), row_blk(pool_width // lru_width)),
            pl.BlockSpec((ts, lru_width), row_blk(pool_width // lru_width + 1)),
            pl.BlockSpec((ts, d_model), row_blk(0)),
        ] + [full(a) for a in consts],
        out_specs=pl.BlockSpec((ts, d_model), row_blk(0)),
        out_shape=jax.ShapeDtypeStruct((t, d_model), F32),
        scratch_shapes=[
            pltpu.VMEM((POOL_TAIL, pool_width), F32),
            pltpu.VMEM((CONV_TAIL, lru_width), F32),
            pltpu.VMEM((SUBLANES, lru_width), F32),
            pltpu.VMEM((ts, lru_width), F32),
            pltpu.VMEM((ts, lru_width), F32),
            pltpu.VMEM((ts, lru_width), F32),
        ],
        compiler_params=pltpu.CompilerParams(
            dimension_semantics=("arbitrary", "arbitrary"),
            vmem_limit_bytes=48 * 1024 * 1024),
        name="mixer",
    )(proj, proj, proj, x2d, *consts)


def _topk_rows(s, k, payload=None):
    n = s.shape[0]
    row = lax.broadcasted_iota(jnp.int32, s.shape, 0).astype(F32)
    vals, idxs, pays = [], [], []
    for _ in range(k):
        m = jnp.max(s, axis=0, keepdims=True)
        idx = jnp.min(jnp.where(s == m, row, float(n)), axis=0, keepdims=True)
        hit = row == idx
        vals.append(m)
        idxs.append(idx)
        if payload is not None:
            pays.append(jnp.sum(jnp.where(hit, payload, 0), axis=0, keepdims=True))
        s = jnp.where(hit, -jnp.inf, s)
    out = (jnp.concatenate(vals, axis=0), jnp.concatenate(idxs, axis=0).astype(jnp.int32))
    if payload is not None:
        out += (jnp.concatenate(pays, axis=0),)
    return out


def _pair_candidates(a_vals, b_vals, k):
    blocks = []
    a = 0
    while k // (a + 1) > 1:
        nb = min(k, -(-(k // (a + 1)) // SUBLANES) * SUBLANES)
        blocks.append(a_vals[a:a + 1, :] + b_vals[0:nb, :])
        a += 1
    blocks.append(a_vals[a:k, :] + b_vals[0:1, :])
    return jnp.concatenate(blocks, axis=0)


def _topk_kernel(q_ref, k1_ref, k2_ref, exp_ref, gate_ref):
    half = k1_ref.shape[2]
    nt = (((1,), (1,)), ((), ()))

    def head(h, carry):
        c0 = pl.multiple_of(h * 2 * half, 2 * half)
        q1 = q_ref[:, pl.ds(c0, half)].astype(BF16)
        q2 = q_ref[:, pl.ds(c0 + half, half)].astype(BF16)
        s1 = lax.dot_general(k1_ref[h], q1, nt, preferred_element_type=F32)
        s2 = lax.dot_general(k2_ref[h], q2, nt, preferred_element_type=F32)
        v1, i1 = _topk_rows(s1, PEER_TOPK)
        v2, i2 = _topk_rows(s2, PEER_TOPK)
        cand = _pair_candidates(v1, v2, PEER_TOPK)
        cidx = _pair_candidates(i1 * PEER_N_KEYS, i2, PEER_TOPK)
        top_s, _, expert = _topk_rows(cand, PEER_TOPK, payload=cidx)
        e = jnp.exp(top_s - jnp.max(top_s, axis=0, keepdims=True))
        gate = e / jnp.sum(e, axis=0, keepdims=True)
        r0 = pl.multiple_of(h * PEER_TOPK, PEER_TOPK)
        exp_ref[pl.ds(r0, PEER_TOPK), :] = expert
        gate_ref[pl.ds(r0, PEER_TOPK), :] = gate
        return carry

    lax.fori_loop(0, PEER_HEADS, head, 0)


def _topk(q, k1, k2, *, tb=256):
    t, dq = q.shape
    npairs = PEER_HEADS * PEER_TOPK
    return pl.pallas_call(
        _topk_kernel,
        grid=(t // tb,),
        in_specs=[
            pl.BlockSpec((tb, dq), lambda i: (i, 0)),
            pl.BlockSpec(k1.shape, lambda i: (0, 0, 0)),
            pl.BlockSpec(k2.shape, lambda i: (0, 0, 0)),
        ],
        out_specs=[
            pl.BlockSpec((npairs, tb), lambda i: (0, i)),
            pl.BlockSpec((npairs, tb), lambda i: (0, i)),
        ],
        out_shape=[
            jax.ShapeDtypeStruct((npairs, t), jnp.int32),
            jax.ShapeDtypeStruct((npairs, t), F32),
        ],
        compiler_params=pltpu.CompilerParams(dimension_semantics=("arbitrary",)),
        name="peer_topk",
    )(q, k1, k2)


PEER_GROUP = 4
PEER_PHASES = 4
PEER_SLOTS = PEER_GROUP * PEER_PHASES
PEER_DMA_QUEUES = 2
PEER_LOOKAHEAD = PEER_GROUP * PEER_PHASES


def _peer_kernel(idx_hbm, gate_ref, h_ref, g2_ref, gf_ref, tab_hbm, o_ref,
                 idx_smem, idx_sems, *scratch):
    tbp = h_ref.shape[0]
    bufs = scratch[:PEER_PHASES]
    sems, z_scr, p_scr = scratch[PEER_PHASES:]
    nchunk = bufs[0].shape[1]
    npairs = bufs[0].shape[2]
    nhalf = nchunk // 2
    blk = tbp * npairs
    win = (tbp + PEER_LOOKAHEAD) * npairs
    ahead = PEER_PHASES - 1
    ngroups = tbp // PEER_GROUP
    assert ngroups % PEER_PHASES == 0
    step = pl.program_id(0)
    nsteps = pl.num_programs(0)
    base = lax.rem(step, 2) * win

    def idx_copy(s):
        half = lax.rem(s, 2)
        return pltpu.make_async_copy(
            idx_hbm.at[pl.ds(pl.multiple_of(s * blk, blk), win)],
            idx_smem.at[pl.ds(pl.multiple_of(half * win, win), win)],
            idx_sems.at[half])

    def issue(t, phase, i):
        for k in range(npairs):
            e = idx_smem[base + t * npairs + k]
            pltpu.make_async_copy(tab_hbm.at[e], bufs[phase].at[i, :, k, :],
                                  sems.at[phase * PEER_GROUP + i]).start(priority=k % PEER_DMA_QUEUES)

    def wait(phase, i):
        pltpu.make_async_copy(bufs[phase].at[i], bufs[phase].at[i], sems.at[phase * PEER_GROUP + i]).wait()

    def issue_group(g, phase):
        for i in range(PEER_GROUP):
            issue(g * PEER_GROUP + i, phase, i)

    def wait_group(phase):
        for i in range(PEER_GROUP):
            wait(phase, i)

    @pl.when(step == 0)
    def _():
        idx_copy(step).start()

    idx_copy(step).wait()

    @pl.when(step + 1 < nsteps)
    def _():
        idx_copy(step + 1).start()

    @pl.when(step == 0)
    def _():
        for g0 in range(ahead):
            issue_group(g0, g0)

    h1 = h_ref[...]
    z_scr[...] = _rmsnorm(h1, g2_ref[...])
    half_w = nhalf * LANES
    lane2 = lax.broadcasted_iota(jnp.int32, (1, 2 * npairs), 1)
    even = (lane2 & 1) == 0
    pad_rows = jnp.zeros((SUBLANES - 4, 2 * npairs), BF16)
    nt = (((1,), (1,)), ((), ()))

    def split(x):
        hi = x.astype(BF16)
        return hi, (x - hi.astype(F32)).astype(BF16)

    def neuron_inputs(t, rows):
        zh, zl = split(z_scr[pl.ds(t, 1), :])
        zpad = jnp.zeros((SUBLANES - 4, half_w), BF16)
        lhs = jnp.concatenate([zh[:, :half_w], zh[:, half_w:], zl[:, :half_w], zl[:, half_w:], zpad], axis=0)
        uw = jnp.concatenate([pltpu.bitcast(rows[j], BF16) for j in range(nhalf)], axis=1)
        return lax.dot_general(lhs, uw, nt, preferred_element_type=F32)

    def neuron_weights(t, acc):
        part = jnp.where(even, acc[0:1] + acc[2:3], acc[1:2] + acc[3:4])
        other = jnp.where(even, pltpu.roll(part, 2 * npairs - 1, axis=1), pltpu.roll(part, 1, axis=1))
        w = gate_ref[pl.ds(t, 1), :] * _gelu_tanh(part + other)
        wh, wl = split(w)
        zero = jnp.zeros_like(wh)
        return jnp.concatenate([jnp.where(even, wh, zero), jnp.where(even, zero, wh),
                                jnp.where(even, wl, zero), jnp.where(even, zero, wl), pad_rows], axis=0)

    def neuron_outputs(t, rows, lhs):
        vw = jnp.concatenate([pltpu.bitcast(rows[nhalf + j], BF16) for j in range(nhalf)], axis=1)
        r = jnp.dot(lhs, vw, preferred_element_type=F32)
        p_scr[pl.ds(t, 1), :] = jnp.concatenate([r[0:1] + r[2:3], r[1:2] + r[3:4]], axis=1)

    def round_(r, carry):
        for phase in range(PEER_PHASES):
            g = r * PEER_PHASES + phase
            wait_group(phase)
            toks = [(g * PEER_GROUP + i, bufs[phase].at[i]) for i in range(PEER_GROUP)]
            nxt = (phase + ahead) % PEER_PHASES
            accs = []
            for i, (t, rows) in enumerate(toks):
                issue((g + ahead) * PEER_GROUP + i, nxt, i)
                accs.append(neuron_inputs(t, rows))
            lhss = [neuron_weights(t, acc) for (t, _), acc in zip(toks, accs)]
            for (t, rows), lhs in zip(toks, lhss):
                neuron_outputs(t, rows, lhs)
        return carry

    lax.fori_loop(0, ngroups // PEER_PHASES, round_, 0)

    @pl.when(step == nsteps - 1)
    def _():
        for g0 in range(ahead):
            wait_group(g0)

    o_ref[...] = _rmsnorm(h1 + p_scr[...], gf_ref[...])


def _peer(idx_flat, gate2, h1, g2, gf, table, *, tbp=128):
    t, d = h1.shape
    npairs = PEER_HEADS * PEER_TOPK
    nchunk = table.shape[1]
    return pl.pallas_call(
        _peer_kernel,
        grid=(t // tbp,),
        in_specs=[
            pl.BlockSpec(memory_space=pl.ANY),
            pl.BlockSpec((tbp, 2 * npairs), lambda i: (i, 0)),
            pl.BlockSpec((tbp, d), lambda i: (i, 0)),
            pl.BlockSpec((1, d), lambda i: (0, 0)),
            pl.BlockSpec((1, d), lambda i: (0, 0)),
            pl.BlockSpec(memory_space=pl.ANY),
        ],
        out_specs=pl.BlockSpec((tbp, d), lambda i: (i, 0)),
        out_shape=jax.ShapeDtypeStruct((t, d), F32),
        scratch_shapes=[
            pltpu.SMEM((2 * (tbp + PEER_LOOKAHEAD) * npairs,), jnp.int32),
            pltpu.SemaphoreType.DMA((2,)),
        ] + [pltpu.VMEM((PEER_GROUP, nchunk, npairs, LANES), jnp.uint32)] * PEER_PHASES + [
            pltpu.SemaphoreType.DMA((PEER_SLOTS,)),
            pltpu.VMEM((tbp, d), F32),
            pltpu.VMEM((tbp, d), F32),
        ],
        compiler_params=pltpu.CompilerParams(
            dimension_semantics=("arbitrary",),
            vmem_limit_bytes=40 * 1024 * 1024),
        name="peer_experts",
    )(idx_flat, gate2, h1, g2.reshape(1, d), gf.reshape(1, d), table)


def _pack_expert_table(peer_u, peer_v):
    n, d = peer_u.shape

    def words(w):
        b = lax.bitcast_convert_type(w.astype(BF16), jnp.uint16).astype(jnp.uint32)
        return b[:, :d // 2] | (b[:, d // 2:] << 16)

    return jnp.concatenate([words(peer_u), words(peer_v)], axis=1).reshape(n, d // LANES, LANES)


def kernel(x, norm1_g, w_in, pool_w, pool_b, pool_scale, conv_w, conv_b, gate_a_w, gate_a_b,
           gate_x_w, gate_x_b, lru_lambda, w_out, norm2_g, peer_wq, peer_keys1, peer_keys2,
           peer_u, peer_v, norm_f_g):
    batch, seq, d_model = x.shape
    assert w_in.shape[0] == 1, "the final norm is fused into the single layer's PEER kernel"
    npairs = PEER_HEADS * PEER_TOPK
    h = x.reshape(batch * seq, d_model)
    for l in range(1):
        proj = _norm_matmul(h, norm1_g[l], w_in[l].astype(BF16))
        h = _mixer(proj, h, batch, seq, pool_w[l].astype(BF16), pool_b[l].reshape(-1), pool_scale[l],
                   conv_w[l], conv_b[l], gate_a_w[l].astype(BF16), gate_a_b[l].reshape(-1),
                   gate_x_w[l].astype(BF16), gate_x_b[l].reshape(-1), lru_lambda[l],
                   w_out[l].astype(BF16))
        q = _norm_matmul(h, norm2_g[l], peer_wq[l].astype(BF16))
        expert, gate = _topk(q, peer_keys1[l].astype(BF16), peer_keys2[l].astype(BF16))
        idx_flat = jnp.pad(expert.T.reshape(-1), (0, PEER_LOOKAHEAD * npairs))
        gate2 = jnp.repeat(gate.T, 2, axis=1)
        table = _pack_expert_table(peer_u[l], peer_v[l])
        h = _peer(idx_flat, gate2, h, norm2_g[l], norm_f_g, table)
    return h.reshape(batch, seq, d_model)
```

```python
import functools

import jax
import jax.numpy as jnp
from jax import lax
from jax.experimental import pallas as pl
from jax.experimental.pallas import tpu as pltpu

EPS = 1e-6
POOL_WINDOWS = (2, 4, 8, 16)
POOL_TAIL = 16
CONV_WIDTH = 4
CONV_TAIL = 8
LRU_C = 8.0
LRU_HEADS = 8
PEER_HEADS = 8
PEER_TOPK = 16
PEER_N_KEYS = 128

LANES = 128
SUBLANES = 8

F32 = jnp.float32
BF16 = jnp.bfloat16


def _rmsnorm(x, g):
    return x * lax.rsqrt(jnp.mean(x * x, axis=-1, keepdims=True) + EPS) * g


def _gelu_tanh(x):
    c = 0.7978845608028654
    return x * (0.5 * (1.0 + jnp.tanh(c * (x + 0.044715 * (x * x * x)))))


def _norm_matmul_kernel(x_ref, g_ref, w_ref, o_ref):
    z = _rmsnorm(x_ref[...], g_ref[...])
    o_ref[...] = jnp.dot(z.astype(BF16), w_ref[...], preferred_element_type=F32)


def _norm_matmul(x, g, w, *, tm=512, tn=1024):
    m, d = x.shape
    n = w.shape[1]
    return pl.pallas_call(
        _norm_matmul_kernel,
        grid=(n // tn, m // tm),
        in_specs=[
            pl.BlockSpec((tm, d), lambda j, i: (i, 0)),
            pl.BlockSpec((1, d), lambda j, i: (0, 0)),
            pl.BlockSpec((d, tn), lambda j, i: (0, j)),
        ],
        out_specs=pl.BlockSpec((tm, tn), lambda j, i: (i, j)),
        out_shape=jax.ShapeDtypeStruct((m, n), F32),
        compiler_params=pltpu.CompilerParams(
            dimension_semantics=("arbitrary", "arbitrary"),
            vmem_limit_bytes=40 * 1024 * 1024),
        name="norm_matmul",
    )(x, g.reshape(1, d), w)


def _mixer_kernel(up_ref, xl_ref, gl_ref, x_ref, pool_w_ref, pool_b_ref, pool_s_ref,
                  conv_w_ref, conv_b_ref, ga_w_ref, ga_b_ref, gx_w_ref, gx_b_ref,
                  lam_ref, w_out_ref, o_ref,
                  pool_tail, conv_tail, h_carry, a_scr, b_scr, h_scr):
    ts = up_ref.shape[0]
    pool_width = up_ref.shape[1]
    lru_width = xl_ref.shape[1]
    pool_group = pool_width // len(POOL_WINDOWS)
    head_dim = lru_width // LRU_HEADS
    s = pl.program_id(1)

    @pl.when(s == 0)
    def _():
        pool_tail[...] = jnp.zeros_like(pool_tail)
        conv_tail[...] = jnp.zeros_like(conv_tail)
        h_carry[...] = jnp.zeros_like(h_carry)

    u = up_ref[...]
    ext = jnp.concatenate([pool_tail[...], u], axis=0)
    pool_tail[...] = u[ts - POOL_TAIL:, :]
    pos = (s * ts + 1 + lax.broadcasted_iota(jnp.int32, (ts, 1), 0)).astype(F32)
    ys = []
    for g, w in enumerate(POOL_WINDOWS):
        acc = ext[:, g * pool_group:(g + 1) * pool_group]
        width = 1
        while width < w:
            acc = acc[width:, :] + acc[:-width, :]
            width *= 2
        start = POOL_TAIL + 1 - w
        win = acc[start:start + ts, :]
        d = win / jnp.minimum(pos, float(w)) - u[:, g * pool_group:(g + 1) * pool_group]
        ys.append(jnp.dot(d.astype(BF16), pool_w_ref[g], preferred_element_type=F32))
    y_pool = (jnp.concatenate(ys, axis=1) + pool_b_ref[...]) * pool_s_ref[...]

    xb = xl_ref[...]
    extx = jnp.concatenate([conv_tail[...], xb], axis=0)
    conv_tail[...] = xb[ts - CONV_TAIL:, :]
    xc = jnp.broadcast_to(conv_b_ref[...], xb.shape)
    for k in range(CONV_WIDTH):
        off = CONV_TAIL - (CONV_WIDTH - 1) + k
        xc = xc + extx[off:off + ts, :] * conv_w_ref[k:k + 1, :]
    xcb = xc.astype(BF16)
    rs, is_ = [], []
    for h in range(LRU_HEADS):
        xh = xcb[:, h * head_dim:(h + 1) * head_dim]
        rs.append(jnp.dot(xh, ga_w_ref[h], preferred_element_type=F32))
        is_.append(jnp.dot(xh, gx_w_ref[h], preferred_element_type=F32))
    r = jax.nn.sigmoid(jnp.concatenate(rs, axis=1) + ga_b_ref[...])
    i_gate = jax.nn.sigmoid(jnp.concatenate(is_, axis=1) + gx_b_ref[...])
    lam = lam_ref[...]
    log_sig = jnp.minimum(lam, 0.0) - jnp.log1p(jnp.exp(-jnp.abs(lam)))
    log_a = (LRU_C * r) * log_sig
    a_scr[...] = jnp.exp(log_a)
    th = jnp.tanh(log_a)
    b_scr[...] = jnp.sqrt(-2.0 * th / (1.0 - th)) * (i_gate * xc)

    row = lax.broadcasted_iota(jnp.int32, (SUBLANES, lru_width), 0)

    def scan_group(gi, h_prev):
        r0 = pl.multiple_of(gi * SUBLANES, SUBLANES)
        a = a_scr[pl.ds(r0, SUBLANES), :]
        b = b_scr[pl.ds(r0, SUBLANES), :]
        for d in (1, 2, 4):
            a_sh = jnp.where(row >= d, pltpu.roll(a, d, axis=0), 1.0)
            b_sh = jnp.where(row >= d, pltpu.roll(b, d, axis=0), 0.0)
            b = a * b_sh + b
            a = a * a_sh
        hg = a * h_prev + b
        h_scr[pl.ds(r0, SUBLANES), :] = hg
        return hg[SUBLANES - 1:SUBLANES, :]

    h_last = lax.fori_loop(0, ts // SUBLANES, scan_group, h_carry[0:1, :])
    h_carry[0:1, :] = h_last
    y_lru = h_scr[...] * _gelu_tanh(gl_ref[...])

    y = jnp.concatenate([y_pool, y_lru], axis=1).astype(BF16)
    o_ref[...] = x_ref[...] + jnp.dot(y, w_out_ref[...], preferred_element_type=F32)


def _mixer(proj, x2d, batch, seq, pool_w, pool_b, pool_scale, conv_w, conv_b,
           ga_w, ga_b, gx_w, gx_b, lam, w_out, *, ts=256):
    t, d_model = x2d.shape
    pool_width = pool_b.shape[-1]
    lru_width = lam.shape[-1]
    ns = seq // ts
    row_blk = lambda c: (lambda b, s: (b * ns + s, c))
    full = lambda a: pl.BlockSpec(a.shape, lambda b, s: (0,) * a.ndim)
    consts = [pool_w, pool_b.reshape(1, -1), pool_scale.reshape(1, -1), conv_w,
              conv_b.reshape(1, -1), ga_w, ga_b.reshape(1, -1), gx_w, gx_b.reshape(1, -1),
              lam.reshape(1, -1), w_out]
    return pl.pallas_call(
        _mixer_kernel,
        grid=(batch, ns),
        in_specs=[
            pl.BlockSpec((ts, pool_width), row_blk(0)),
            pl.BlockSpec((ts, lru_width), row_blk(pool_width // lru_width)),
            pl.BlockSpec((ts, lru_width), row_blk(pool_width // lru_width + 1)),
            pl.BlockSpec((ts, d_model), row_blk(0)),
        ] + [full(a) for a in consts],
        out_specs=pl.BlockSpec((ts, d_model), row_blk(0)),
        out_shape=jax.ShapeDtypeStruct((t, d_model), F32),
        scratch_shapes=[
            pltpu.VMEM((POOL_TAIL, pool_width), F32),
            pltpu.VMEM((CONV_TAIL, lru_width), F32),
            pltpu.VMEM((SUBLANES, lru_width), F32),
            pltpu.VMEM((ts, lru_width), F32),
            pltpu.VMEM((ts, lru_width), F32),
            pltpu.VMEM((ts, lru_width), F32),
        ],
        compiler_params=pltpu.CompilerParams(
            dimension_semantics=("arbitrary", "arbitrary"),
            vmem_limit_bytes=48 * 1024 * 1024),
        name="mixer",
    )(proj, proj, proj, x2d, *consts)


def _topk_rows(s, k, payload=None):
    n = s.shape[0]
    row = lax.broadcasted_iota(jnp.int32, s.shape, 0).astype(F32)
    vals, idxs, pays = [], [], []
    for _ in range(k):
        m = jnp.max(s, axis=0, keepdims=True)
        idx = jnp.min(jnp.where(s == m, row, float(n)), axis=0, keepdims=True)
        hit = row == idx
        vals.append(m)
        idxs.append(idx)
        if payload is not None:
            pays.append(jnp.sum(jnp.where(hit, payload, 0), axis=0, keepdims=True))
        s = jnp.where(hit, -jnp.inf, s)
    out = (jnp.concatenate(vals, axis=0), jnp.concatenate(idxs, axis=0).astype(jnp.int32))
    if payload is not None:
        out += (jnp.concatenate(pays, axis=0),)
    return out


def _pair_candidates(a_vals, b_vals, k):
    blocks = []
    a = 0
    while k // (a + 1) > 1:
        nb = min(k, -(-(k // (a + 1)) // SUBLANES) * SUBLANES)
        blocks.append(a_vals[a:a + 1, :] + b_vals[0:nb, :])
        a += 1
    blocks.append(a_vals[a:k, :] + b_vals[0:1, :])
    return jnp.concatenate(blocks, axis=0)


def _topk_kernel(q_ref, k1_ref, k2_ref, exp_ref, gate_ref):
    half = k1_ref.shape[2]
    nt = (((1,), (1,)), ((), ()))

    def head(h, carry):
        c0 = pl.multiple_of(h * 2 * half, 2 * half)
        q1 = q_ref[:, pl.ds(c0, half)].astype(BF16)
        q2 = q_ref[:, pl.ds(c0 + half, half)].astype(BF16)
        s1 = lax.dot_general(k1_ref[h], q1, nt, preferred_element_type=F32)
        s2 = lax.dot_general(k2_ref[h], q2, nt, preferred_element_type=F32)
        v1, i1 = _topk_rows(s1, PEER_TOPK)
        v2, i2 = _topk_rows(s2, PEER_TOPK)
        cand = _pair_candidates(v1, v2, PEER_TOPK)
        cidx = _pair_candidates(i1 * PEER_N_KEYS, i2, PEER_TOPK)
        top_s, _, expert = _topk_rows(cand, PEER_TOPK, payload=cidx)
        e = jnp.exp(top_s - jnp.max(top_s, axis=0, keepdims=True))
        gate = e / jnp.sum(e, axis=0, keepdims=True)
        r0 = pl.multiple_of(h * PEER_TOPK, PEER_TOPK)
        exp_ref[pl.ds(r0, PEER_TOPK), :] = expert
        gate_ref[pl.ds(r0, PEER_TOPK), :] = gate
        return carry

    lax.fori_loop(0, PEER_HEADS, head, 0)


def _topk(q, k1, k2, *, tb=256):
    t, dq = q.shape
    npairs = PEER_HEADS * PEER_TOPK
    return pl.pallas_call(
        _topk_kernel,
        grid=(t // tb,),
        in_specs=[
            pl.BlockSpec((tb, dq), lambda i: (i, 0)),
            pl.BlockSpec(k1.shape, lambda i: (0, 0, 0)),
            pl.BlockSpec(k2.shape, lambda i: (0, 0, 0)),
        ],
        out_specs=[
            pl.BlockSpec((npairs, tb), lambda i: (0, i)),
            pl.BlockSpec((npairs, tb), lambda i: (0, i)),
        ],
        out_shape=[
            jax.ShapeDtypeStruct((npairs, t), jnp.int32),
            jax.ShapeDtypeStruct((npairs, t), F32),
        ],
        compiler_params=pltpu.CompilerParams(dimension_semantics=("arbitrary",)),
        name="peer_topk",
    )(q, k1, k2)


PEER_GROUP = 4
PEER_PHASES = 4
PEER_SLOTS = PEER_GROUP * PEER_PHASES
PEER_DMA_QUEUES = 2
PEER_LOOKAHEAD = PEER_GROUP * PEER_PHASES


def _peer_kernel(idx_hbm, gate_ref, h_ref, g2_ref, gf_ref, tab_hbm, o_ref,
                 idx_smem, idx_sems, *scratch):
    tbp = h_ref.shape[0]
    bufs = scratch[:PEER_PHASES]
    sems, z_scr, p_scr = scratch[PEER_PHASES:]
    nsets, piece = tab_hbm.shape[1], tab_hbm.shape[2]
    npairs = bufs[0].shape[2] // piece
    usets = nsets // 2
    rpp = 2 * piece
    nlane = npairs * rpp
    assert 2 * rpp == SUBLANES
    blk = tbp * npairs
    win = (tbp + PEER_LOOKAHEAD) * npairs
    ahead = PEER_PHASES - 1
    ngroups = tbp // PEER_GROUP
    assert ngroups % PEER_PHASES == 0
    step = pl.program_id(0)
    nsteps = pl.num_programs(0)
    base = lax.rem(step, 2) * win

    def idx_copy(s):
        half = lax.rem(s, 2)
        return pltpu.make_async_copy(
            idx_hbm.at[pl.ds(pl.multiple_of(s * blk, blk), win)],
            idx_smem.at[pl.ds(pl.multiple_of(half * win, win), win)],
            idx_sems.at[half])

    def issue(t, phase, i):
        for k in range(npairs):
            e = idx_smem[base + t * npairs + k]
            pltpu.make_async_copy(tab_hbm.at[e], bufs[phase].at[i, :, pl.ds(piece * k, piece), :],
                                  sems.at[phase * PEER_GROUP + i]).start(priority=k % PEER_DMA_QUEUES)

    def wait(phase, i):
        pltpu.make_async_copy(bufs[phase].at[i], bufs[phase].at[i], sems.at[phase * PEER_GROUP + i]).wait()

    def issue_group(g, phase):
        for i in range(PEER_GROUP):
            issue(g * PEER_GROUP + i, phase, i)

    def wait_group(phase):
        for i in range(PEER_GROUP):
            wait(phase, i)

    @pl.when(step == 0)
    def _():
        idx_copy(step).start()

    idx_copy(step).wait()

    @pl.when(step + 1 < nsteps)
    def _():
        idx_copy(step + 1).start()

    @pl.when(step == 0)
    def _():
        for g0 in range(ahead):
            issue_group(g0, g0)

    h1 = h_ref[...]
    z_scr[...] = _rmsnorm(h1, g2_ref[...])
    half_w = usets * piece * LANES
    lane_n = lax.broadcasted_iota(jnp.int32, (1, nlane), 1)
    sub = lane_n & (rpp - 1)
    nt = (((1,), (1,)), ((), ()))

    def split(x):
        hi = x.astype(BF16)
        return hi, (x - hi.astype(F32)).astype(BF16)

    def feature_start(s, r):
        c, p = divmod(r, 2)
        return p * half_w + (s * piece + c) * LANES

    def neuron_inputs(t, rows):
        zh, zl = split(z_scr[pl.ds(t, 1), :])

        def lhs_rows(zz):
            return [jnp.concatenate([zz[:, feature_start(s, r):feature_start(s, r) + LANES]
                                     for s in range(usets)], axis=1) for r in range(rpp)]

        lhs = jnp.concatenate(lhs_rows(zh) + lhs_rows(zl), axis=0)
        uw = jnp.concatenate([pltpu.bitcast(rows[s], BF16) for s in range(usets)], axis=1)
        return lax.dot_general(lhs, uw, nt, preferred_element_type=F32)

    def neuron_weights(t, acc):
        tot = acc[0:rpp] + acc[rpp:2 * rpp]
        act = jnp.where(sub == 0, tot[0:1], 0.0)
        for r in range(1, rpp):
            act = jnp.where(sub == r, tot[r:r + 1], act)
        d = 1
        while d < rpp:
            act = act + jnp.where((lane_n & d) == 0, pltpu.roll(act, nlane - d, axis=1),
                                  pltpu.roll(act, d, axis=1))
            d *= 2
        w = gate_ref[pl.ds(t, 1), :] * _gelu_tanh(act)
        wh, wl = split(w)
        zero = jnp.zeros_like(wh)
        return jnp.concatenate([jnp.where(sub == r, wh, zero) for r in range(rpp)]
                               + [jnp.where(sub == r, wl, zero) for r in range(rpp)], axis=0)

    def neuron_outputs(t, rows, lhs):
        vw = jnp.concatenate([pltpu.bitcast(rows[usets + s], BF16) for s in range(usets)], axis=1)
        res = jnp.dot(lhs, vw, preferred_element_type=F32)
        tot = res[0:rpp] + res[rpp:2 * rpp]
        pieces = {feature_start(s, r): tot[r:r + 1, s * LANES:(s + 1) * LANES]
                  for s in range(usets) for r in range(rpp)}
        p_scr[pl.ds(t, 1), :] = jnp.concatenate([pieces[f] for f in sorted(pieces)], axis=1)

    def round_(r, carry):
        for phase in range(PEER_PHASES):
            g = r * PEER_PHASES + phase
            wait_group(phase)
            toks = [(g * PEER_GROUP + i, bufs[phase].at[i]) for i in range(PEER_GROUP)]
            nxt = (phase + ahead) % PEER_PHASES
            accs = []
            for i, (t, rows) in enumerate(toks):
                issue((g + ahead) * PEER_GROUP + i, nxt, i)
                accs.append(neuron_inputs(t, rows))
            lhss = [neuron_weights(t, acc) for (t, _), acc in zip(toks, accs)]
            for (t, rows), lhs in zip(toks, lhss):
                neuron_outputs(t, rows, lhs)
        return carry

    lax.fori_loop(0, ngroups // PEER_PHASES, round_, 0)

    @pl.when(step == nsteps - 1)
    def _():
        for g0 in range(ahead):
            wait_group(g0)

    o_ref[...] = _rmsnorm(h1 + p_scr[...], gf_ref[...])


def _peer(idx_flat, gate_rep, h1, g2, gf, table, *, tbp=128):
    t, d = h1.shape
    npairs = PEER_HEADS * PEER_TOPK
    nsets, piece = table.shape[1], table.shape[2]
    return pl.pallas_call(
        _peer_kernel,
        grid=(t // tbp,),
        in_specs=[
            pl.BlockSpec(memory_space=pl.ANY),
            pl.BlockSpec((tbp, gate_rep.shape[1]), lambda i: (i, 0)),
            pl.BlockSpec((tbp, d), lambda i: (i, 0)),
            pl.BlockSpec((1, d), lambda i: (0, 0)),
            pl.BlockSpec((1, d), lambda i: (0, 0)),
            pl.BlockSpec(memory_space=pl.ANY),
        ],
        out_specs=pl.BlockSpec((tbp, d), lambda i: (i, 0)),
        out_shape=jax.ShapeDtypeStruct((t, d), F32),
        scratch_shapes=[
            pltpu.SMEM((2 * (tbp + PEER_LOOKAHEAD) * npairs,), jnp.int32),
            pltpu.SemaphoreType.DMA((2,)),
        ] + [pltpu.VMEM((PEER_GROUP, nsets, npairs * piece, LANES), jnp.uint32)] * PEER_PHASES + [
            pltpu.SemaphoreType.DMA((PEER_SLOTS,)),
            pltpu.VMEM((tbp, d), F32),
            pltpu.VMEM((tbp, d), F32),
        ],
        compiler_params=pltpu.CompilerParams(
            dimension_semantics=("arbitrary",),
            vmem_limit_bytes=40 * 1024 * 1024),
        name="peer_experts",
    )(idx_flat, gate_rep, h1, g2.reshape(1, d), gf.reshape(1, d), table)


PEER_PIECE = 2


def _pack_expert_table(peer_u, peer_v):
    n, d = peer_u.shape

    def words(w):
        b = lax.bitcast_convert_type(w.astype(BF16), jnp.uint16).astype(jnp.uint32)
        return b[:, :d // 2] | (b[:, d // 2:] << 16)

    return jnp.concatenate([words(peer_u), words(peer_v)], axis=1).reshape(
        n, d // (LANES * PEER_PIECE), PEER_PIECE, LANES)


def kernel(x, norm1_g, w_in, pool_w, pool_b, pool_scale, conv_w, conv_b, gate_a_w, gate_a_b,
           gate_x_w, gate_x_b, lru_lambda, w_out, norm2_g, peer_wq, peer_keys1, peer_keys2,
           peer_u, peer_v, norm_f_g):
    batch, seq, d_model = x.shape
    assert w_in.shape[0] == 1, "the final norm is fused into the single layer's PEER kernel"
    npairs = PEER_HEADS * PEER_TOPK
    h = x.reshape(batch * seq, d_model)
    for l in range(1):
        proj = _norm_matmul(h, norm1_g[l], w_in[l].astype(BF16))
        h = _mixer(proj, h, batch, seq, pool_w[l].astype(BF16), pool_b[l].reshape(-1), pool_scale[l],
                   conv_w[l], conv_b[l], gate_a_w[l].astype(BF16), gate_a_b[l].reshape(-1),
                   gate_x_w[l].astype(BF16), gate_x_b[l].reshape(-1), lru_lambda[l],
                   w_out[l].astype(BF16))
        q = _norm_matmul(h, norm2_g[l], peer_wq[l].astype(BF16))
        expert, gate = _topk(q, peer_keys1[l].astype(BF16), peer_keys2[l].astype(BF16))
        idx_flat = jnp.pad(expert.T.reshape(-1), (0, PEER_LOOKAHEAD * npairs))
        gate_rep = jnp.repeat(gate.T, 2 * PEER_PIECE, axis=1)
        table = _pack_expert_table(peer_u[l], peer_v[l])
        h = _peer(idx_flat, gate_rep, h, norm2_g[l], norm_f_g, table)
    return h.reshape(batch, seq, d_model)
```

```python
import functools

import jax
import jax.numpy as jnp
from jax import lax
from jax.experimental import pallas as pl
from jax.experimental.pallas import tpu as pltpu

EPS = 1e-6
POOL_WINDOWS = (2, 4, 8, 16)
POOL_TAIL = 16
CONV_WIDTH = 4
CONV_TAIL = 8
LRU_C = 8.0
LRU_HEADS = 8
PEER_HEADS = 8
PEER_TOPK = 16
PEER_N_KEYS = 128

LANES = 128
SUBLANES = 8

F32 = jnp.float32
BF16 = jnp.bfloat16


def _rmsnorm(x, g):
    return x * lax.rsqrt(jnp.mean(x * x, axis=-1, keepdims=True) + EPS) * g


def _gelu_tanh(x):
    c = 0.7978845608028654
    return x * (0.5 * (1.0 + jnp.tanh(c * (x + 0.044715 * (x * x * x)))))


def _norm_matmul_kernel(x_ref, g_ref, w_ref, o_ref):
    z = _rmsnorm(x_ref[...], g_ref[...])
    o_ref[...] = jnp.dot(z.astype(BF16), w_ref[...], preferred_element_type=F32)


def _norm_matmul(x, g, w, *, tm=512, tn=1024):
    m, d = x.shape
    n = w.shape[1]
    return pl.pallas_call(
        _norm_matmul_kernel,
        grid=(n // tn, m // tm),
        in_specs=[
            pl.BlockSpec((tm, d), lambda j, i: (i, 0)),
            pl.BlockSpec((1, d), lambda j, i: (0, 0)),
            pl.BlockSpec((d, tn), lambda j, i: (0, j)),
        ],
        out_specs=pl.BlockSpec((tm, tn), lambda j, i: (i, j)),
        out_shape=jax.ShapeDtypeStruct((m, n), F32),
        compiler_params=pltpu.CompilerParams(
            dimension_semantics=("arbitrary", "arbitrary"),
            vmem_limit_bytes=40 * 1024 * 1024),
        name="norm_matmul",
    )(x, g.reshape(1, d), w)


def _mixer_kernel(up_ref, xl_ref, gl_ref, x_ref, pool_w_ref, pool_b_ref, pool_s_ref,
                  conv_w_ref, conv_b_ref, ga_w_ref, ga_b_ref, gx_w_ref, gx_b_ref,
                  lam_ref, w_out_ref, o_ref,
                  pool_tail, conv_tail, h_carry, a_scr, b_scr, h_scr):
    ts = up_ref.shape[0]
    pool_width = up_ref.shape[1]
    lru_width = xl_ref.shape[1]
    pool_group = pool_width // len(POOL_WINDOWS)
    head_dim = lru_width // LRU_HEADS
    s = pl.program_id(1)

    @pl.when(s == 0)
    def _():
        pool_tail[...] = jnp.zeros_like(pool_tail)
        conv_tail[...] = jnp.zeros_like(conv_tail)
        h_carry[...] = jnp.zeros_like(h_carry)

    u = up_ref[...]
    ext = jnp.concatenate([pool_tail[...], u], axis=0)
    pool_tail[...] = u[ts - POOL_TAIL:, :]
    pos = (s * ts + 1 + lax.broadcasted_iota(jnp.int32, (ts, 1), 0)).astype(F32)
    ys = []
    for g, w in enumerate(POOL_WINDOWS):
        acc = ext[:, g * pool_group:(g + 1) * pool_group]
        width = 1
        while width < w:
            acc = acc[width:, :] + acc[:-width, :]
            width *= 2
        start = POOL_TAIL + 1 - w
        win = acc[start:start + ts, :]
        d = win / jnp.minimum(pos, float(w)) - u[:, g * pool_group:(g + 1) * pool_group]
        ys.append(jnp.dot(d.astype(BF16), pool_w_ref[g], preferred_element_type=F32))
    y_pool = (jnp.concatenate(ys, axis=1) + pool_b_ref[...]) * pool_s_ref[...]

    xb = xl_ref[...]
    extx = jnp.concatenate([conv_tail[...], xb], axis=0)
    conv_tail[...] = xb[ts - CONV_TAIL:, :]
    xc = jnp.broadcast_to(conv_b_ref[...], xb.shape)
    for k in range(CONV_WIDTH):
        off = CONV_TAIL - (CONV_WIDTH - 1) + k
        xc = xc + extx[off:off + ts, :] * conv_w_ref[k:k + 1, :]
    xcb = xc.astype(BF16)
    rs, is_ = [], []
    for h in range(LRU_HEADS):
        xh = xcb[:, h * head_dim:(h + 1) * head_dim]
        rs.append(jnp.dot(xh, ga_w_ref[h], preferred_element_type=F32))
        is_.append(jnp.dot(xh, gx_w_ref[h], preferred_element_type=F32))
    r = jax.nn.sigmoid(jnp.concatenate(rs, axis=1) + ga_b_ref[...])
    i_gate = jax.nn.sigmoid(jnp.concatenate(is_, axis=1) + gx_b_ref[...])
    lam = lam_ref[...]
    log_sig = jnp.minimum(lam, 0.0) - jnp.log1p(jnp.exp(-jnp.abs(lam)))
    log_a = (LRU_C * r) * log_sig
    a_scr[...] = jnp.exp(log_a)
    th = jnp.tanh(log_a)
    b_scr[...] = jnp.sqrt(-2.0 * th / (1.0 - th)) * (i_gate * xc)

    row = lax.broadcasted_iota(jnp.int32, (SUBLANES, lru_width), 0)

    def scan_group(gi, h_prev):
        r0 = pl.multiple_of(gi * SUBLANES, SUBLANES)
        a = a_scr[pl.ds(r0, SUBLANES), :]
        b = b_scr[pl.ds(r0, SUBLANES), :]
        for d in (1, 2, 4):
            a_sh = jnp.where(row >= d, pltpu.roll(a, d, axis=0), 1.0)
            b_sh = jnp.where(row >= d, pltpu.roll(b, d, axis=0), 0.0)
            b = a * b_sh + b
            a = a * a_sh
        hg = a * h_prev + b
        h_scr[pl.ds(r0, SUBLANES), :] = hg
        return hg[SUBLANES - 1:SUBLANES, :]

    h_last = lax.fori_loop(0, ts // SUBLANES, scan_group, h_carry[0:1, :])
    h_carry[0:1, :] = h_last
    y_lru = h_scr[...] * _gelu_tanh(gl_ref[...])

    y = jnp.concatenate([y_pool, y_lru], axis=1).astype(BF16)
    o_ref[...] = x_ref[...] + jnp.dot(y, w_out_ref[...], preferred_element_type=F32)


def _mixer(proj, x2d, batch, seq, pool_w, pool_b, pool_scale, conv_w, conv_b,
           ga_w, ga_b, gx_w, gx_b, lam, w_out, *, ts=256):
    t, d_model = x2d.shape
    pool_width = pool_b.shape[-1]
    lru_width = lam.shape[-1]
    ns = seq // ts
    row_blk = lambda c: (lambda b, s: (b * ns + s, c))
    full = lambda a: pl.BlockSpec(a.shape, lambda b, s: (0,) * a.ndim)
    consts = [pool_w, pool_b.reshape(1, -1), pool_scale.reshape(1, -1), conv_w,
              conv_b.reshape(1, -1), ga_w, ga_b.reshape(1, -1), gx_w, gx_b.reshape(1, -1),
              lam.reshape(1, -1), w_out]
    return pl.pallas_call(
        _mixer_kernel,
        grid=(batch, ns),
        in_specs=[
            pl.BlockSpec((ts, pool_width), row_blk(0)),
            pl.BlockSpec((ts, lru_width), row_blk(pool_width // lru_width)),
            pl.BlockSpec((ts, lru_width), row_blk(pool_width // lru_width + 1)),
            pl.BlockSpec((ts, d_model), row_blk(0)),
        ] + [full(a) for a in consts],
        out_specs=pl.BlockSpec((ts, d_model), row_blk(0)),
        out_shape=jax.ShapeDtypeStruct((t, d_model), F32),
        scratch_shapes=[
            pltpu.VMEM((POOL_TAIL, pool_width), F32),
            pltpu.VMEM((CONV_TAIL, lru_width), F32),
            pltpu.VMEM((SUBLANES, lru_width), F32),
            pltpu.VMEM((ts, lru_width), F32),
            pltpu.VMEM((ts, lru_width), F32),
            pltpu.VMEM((ts, lru_width), F32),
        ],
        compiler_params=pltpu.CompilerParams(
            dimension_semantics=("arbitrary", "arbitrary"),
            vmem_limit_bytes=48 * 1024 * 1024),
        name="mixer",
    )(proj, proj, proj, x2d, *consts)


def _topk_rows(s, k, payload=None):
    n = s.shape[0]
    row = lax.broadcasted_iota(jnp.int32, s.shape, 0).astype(F32)
    vals, idxs, pays = [], [], []
    for _ in range(k):
        m = jnp.max(s, axis=0, keepdims=True)
        idx = jnp.min(jnp.where(s == m, row, float(n)), axis=0, keepdims=True)
        hit = row == idx
        vals.append(m)
        idxs.append(idx)
        if payload is not None:
            pays.append(jnp.sum(jnp.where(hit, payload, 0), axis=0, keepdims=True))
        s = jnp.where(hit, -jnp.inf, s)
    out = (jnp.concatenate(vals, axis=0), jnp.concatenate(idxs, axis=0).astype(jnp.int32))
    if payload is not None:
        out += (jnp.concatenate(pays, axis=0),)
    return out


def _pair_candidates(a_vals, b_vals, k):
    blocks = []
    a = 0
    while k // (a + 1) > 1:
        nb = min(k, -(-(k // (a + 1)) // SUBLANES) * SUBLANES)
        blocks.append(a_vals[a:a + 1, :] + b_vals[0:nb, :])
        a += 1
    blocks.append(a_vals[a:k, :] + b_vals[0:1, :])
    return jnp.concatenate(blocks, axis=0)


def _topk_kernel(q_ref, k1_ref, k2_ref, exp_ref, gate_ref):
    half = k1_ref.shape[2]
    nt = (((1,), (1,)), ((), ()))

    def head(h, carry):
        c0 = pl.multiple_of(h * 2 * half, 2 * half)
        q1 = q_ref[:, pl.ds(c0, half)].astype(BF16)
        q2 = q_ref[:, pl.ds(c0 + half, half)].astype(BF16)
        s1 = lax.dot_general(k1_ref[h], q1, nt, preferred_element_type=F32)
        s2 = lax.dot_general(k2_ref[h], q2, nt, preferred_element_type=F32)
        v1, i1 = _topk_rows(s1, PEER_TOPK)
        v2, i2 = _topk_rows(s2, PEER_TOPK)
        cand = _pair_candidates(v1, v2, PEER_TOPK)
        cidx = _pair_candidates(i1 * PEER_N_KEYS, i2, PEER_TOPK)
        top_s, _, expert = _topk_rows(cand, PEER_TOPK, payload=cidx)
        e = jnp.exp(top_s - jnp.max(top_s, axis=0, keepdims=True))
        gate = e / jnp.sum(e, axis=0, keepdims=True)
        r0 = pl.multiple_of(h * PEER_TOPK, PEER_TOPK)
        exp_ref[pl.ds(r0, PEER_TOPK), :] = expert
        gate_ref[pl.ds(r0, PEER_TOPK), :] = gate
        return carry

    lax.fori_loop(0, PEER_HEADS, head, 0)


def _topk(q, k1, k2, *, tb=256):
    t, dq = q.shape
    npairs = PEER_HEADS * PEER_TOPK
    return pl.pallas_call(
        _topk_kernel,
        grid=(t // tb,),
        in_specs=[
            pl.BlockSpec((tb, dq), lambda i: (i, 0)),
            pl.BlockSpec(k1.shape, lambda i: (0, 0, 0)),
            pl.BlockSpec(k2.shape, lambda i: (0, 0, 0)),
        ],
        out_specs=[
            pl.BlockSpec((npairs, tb), lambda i: (0, i)),
            pl.BlockSpec((npairs, tb), lambda i: (0, i)),
        ],
        out_shape=[
            jax.ShapeDtypeStruct((npairs, t), jnp.int32),
            jax.ShapeDtypeStruct((npairs, t), F32),
        ],
        compiler_params=pltpu.CompilerParams(dimension_semantics=("arbitrary",)),
        name="peer_topk",
    )(q, k1, k2)


PEER_GROUP = 8
PEER_PHASES = 4
PEER_SLOTS = PEER_GROUP * PEER_PHASES
PEER_DMA_QUEUES = 2
PEER_LOOKAHEAD = PEER_GROUP * PEER_PHASES


def _peer_kernel(idx_hbm, gate_ref, h_ref, g2_ref, gf_ref, tab_hbm, o_ref,
                 idx_smem, idx_sems, *scratch):
    tbp = h_ref.shape[0]
    bufs = scratch[:PEER_PHASES]
    sems, z_scr, p_scr = scratch[PEER_PHASES:]
    npairs, nchunk = bufs[0].shape[1], bufs[0].shape[2]
    uchunks = nchunk // 2
    rpp = 2 * uchunks
    assert LANES % rpp == 0 and npairs * rpp % LANES == 0
    ppt = LANES // rpp
    ntile = npairs // ppt
    blk = tbp * npairs
    win = (tbp + PEER_LOOKAHEAD) * npairs
    ahead = PEER_PHASES - 1
    ngroups = tbp // PEER_GROUP
    assert ngroups % PEER_PHASES == 0
    step = pl.program_id(0)
    nsteps = pl.num_programs(0)
    base = lax.rem(step, 2) * win

    def idx_copy(s):
        half = lax.rem(s, 2)
        return pltpu.make_async_copy(
            idx_hbm.at[pl.ds(pl.multiple_of(s * blk, blk), win)],
            idx_smem.at[pl.ds(pl.multiple_of(half * win, win), win)],
            idx_sems.at[half])

    def issue(t, phase, i):
        for k in range(npairs):
            e = idx_smem[base + t * npairs + k]
            pltpu.make_async_copy(tab_hbm.at[e], bufs[phase].at[i, k],
                                  sems.at[phase * PEER_GROUP + i]).start(priority=k % PEER_DMA_QUEUES)

    def wait(phase, i):
        pltpu.make_async_copy(bufs[phase].at[i], bufs[phase].at[i], sems.at[phase * PEER_GROUP + i]).wait()

    def issue_group(g, phase):
        for i in range(PEER_GROUP):
            issue(g * PEER_GROUP + i, phase, i)

    def wait_group(phase):
        for i in range(PEER_GROUP):
            wait(phase, i)

    @pl.when(step == 0)
    def _():
        idx_copy(step).start()

    idx_copy(step).wait()

    @pl.when(step + 1 < nsteps)
    def _():
        idx_copy(step + 1).start()

    @pl.when(step == 0)
    def _():
        for g0 in range(ahead):
            issue_group(g0, g0)

    h1 = h_ref[...]
    z_scr[...] = _rmsnorm(h1, g2_ref[...])
    half_w = uchunks * LANES
    nt = (((1,), (1,)), ((), ()))
    lane = lax.broadcasted_iota(jnp.int32, (rpp, LANES), 1)
    diag = lax.broadcasted_iota(jnp.int32, (rpp, LANES), 0) == (lane & (rpp - 1))

    def split(x):
        hi = x.astype(BF16)
        return hi, (x - hi.astype(F32)).astype(BF16)

    def feature_start(r):
        c, p = divmod(r, 2)
        return p * half_w + c * LANES

    def part_view(rows, first):
        words = rows[:, first:first + uchunks, :]
        return pltpu.bitcast(words.reshape(npairs * uchunks, LANES), BF16)

    def neuron_inputs(t, rows):
        zrow = z_scr[pl.ds(t, 1), :]
        zr = jnp.concatenate([zrow[:, feature_start(r):feature_start(r) + LANES] for r in range(rpp)], axis=0)
        lhs = jnp.concatenate(split(zr), axis=0)
        return lax.dot_general(lhs, part_view(rows, 0), nt, preferred_element_type=F32)

    def neuron_weights(t, acc):
        tot = acc[0:rpp] + acc[rpp:2 * rpp]
        s = jnp.concatenate([jnp.sum(jnp.where(diag, tot[:, j * LANES:(j + 1) * LANES], 0.0), axis=0, keepdims=True)
                             for j in range(ntile)], axis=0)
        lane_s = lax.broadcasted_iota(jnp.int32, s.shape, 1)
        d = 1
        while d < rpp:
            s = s + jnp.where((lane_s & d) == 0, pltpu.roll(s, LANES - d, axis=1), pltpu.roll(s, d, axis=1))
            d *= 2
        w = gate_ref[t] * _gelu_tanh(s)
        wh = w.astype(BF16).astype(F32)
        wl = w - wh
        blocks = []
        for j in range(ntile):
            hi = jnp.where(diag, jnp.broadcast_to(wh[j:j + 1, :], diag.shape), 0.0)
            lo = jnp.where(diag, jnp.broadcast_to(wl[j:j + 1, :], diag.shape), 0.0)
            blocks.append(jnp.concatenate([hi, lo], axis=0).astype(BF16))
        return jnp.concatenate(blocks, axis=1)

    def neuron_outputs(t, rows, lhs):
        res = jnp.dot(lhs, part_view(rows, uchunks), preferred_element_type=F32)
        tot = res[0:rpp] + res[rpp:2 * rpp]
        order = sorted(range(rpp), key=feature_start)
        p_scr[pl.ds(t, 1), :] = jnp.concatenate([tot[r:r + 1, :] for r in order], axis=1)

    def round_(r, carry):
        for phase in range(PEER_PHASES):
            g = r * PEER_PHASES + phase
            wait_group(phase)
            toks = [(g * PEER_GROUP + i, bufs[phase].at[i]) for i in range(PEER_GROUP)]
            nxt = (phase + ahead) % PEER_PHASES
            accs = []
            for i, (t, rows) in enumerate(toks):
                issue((g + ahead) * PEER_GROUP + i, nxt, i)
                accs.append(neuron_inputs(t, rows))
            lhss = [neuron_weights(t, acc) for (t, _), acc in zip(toks, accs)]
            for (t, rows), lhs in zip(toks, lhss):
                neuron_outputs(t, rows, lhs)
        return carry

    lax.fori_loop(0, ngroups // PEER_PHASES, round_, 0)

    @pl.when(step == nsteps - 1)
    def _():
        for g0 in range(ahead):
            wait_group(g0)

    o_ref[...] = _rmsnorm(h1 + p_scr[...], gf_ref[...])


def _peer(idx_flat, gate_rep, h1, g2, gf, table, *, tbp=128):
    t, d = h1.shape
    npairs = PEER_HEADS * PEER_TOPK
    nchunk = table.shape[1]
    return pl.pallas_call(
        _peer_kernel,
        grid=(t // tbp,),
        in_specs=[
            pl.BlockSpec(memory_space=pl.ANY),
            pl.BlockSpec((tbp,) + gate_rep.shape[1:], lambda i: (i, 0, 0)),
            pl.BlockSpec((tbp, d), lambda i: (i, 0)),
            pl.BlockSpec((1, d), lambda i: (0, 0)),
            pl.BlockSpec((1, d), lambda i: (0, 0)),
            pl.BlockSpec(memory_space=pl.ANY),
        ],
        out_specs=pl.BlockSpec((tbp, d), lambda i: (i, 0)),
        out_shape=jax.ShapeDtypeStruct((t, d), F32),
        scratch_shapes=[
            pltpu.SMEM((2 * (tbp + PEER_LOOKAHEAD) * npairs,), jnp.int32),
            pltpu.SemaphoreType.DMA((2,)),
        ] + [pltpu.VMEM((PEER_GROUP, npairs, nchunk, LANES), jnp.uint32)] * PEER_PHASES + [
            pltpu.SemaphoreType.DMA((PEER_SLOTS,)),
            pltpu.VMEM((tbp, d), F32),
            pltpu.VMEM((tbp, d), F32),
        ],
        compiler_params=pltpu.CompilerParams(
            dimension_semantics=("arbitrary",),
            vmem_limit_bytes=56 * 1024 * 1024),
        name="peer_experts",
    )(idx_flat, gate_rep, h1, g2.reshape(1, d), gf.reshape(1, d), table)


def _pack_expert_table(peer_u, peer_v):
    n, d = peer_u.shape

    def words(w):
        b = lax.bitcast_convert_type(w.astype(BF16), jnp.uint16).astype(jnp.uint32)
        return b[:, :d // 2] | (b[:, d // 2:] << 16)

    return jnp.concatenate([words(peer_u), words(peer_v)], axis=1).reshape(n, d // LANES, LANES)


def kernel(x, norm1_g, w_in, pool_w, pool_b, pool_scale, conv_w, conv_b, gate_a_w, gate_a_b,
           gate_x_w, gate_x_b, lru_lambda, w_out, norm2_g, peer_wq, peer_keys1, peer_keys2,
           peer_u, peer_v, norm_f_g):
    batch, seq, d_model = x.shape
    assert w_in.shape[0] == 1, "the final norm is fused into the single layer's PEER kernel"
    npairs = PEER_HEADS * PEER_TOPK
    h = x.reshape(batch * seq, d_model)
    for l in range(1):
        proj = _norm_matmul(h, norm1_g[l], w_in[l].astype(BF16))
        h = _mixer(proj, h, batch, seq, pool_w[l].astype(BF16), pool_b[l].reshape(-1), pool_scale[l],
                   conv_w[l], conv_b[l], gate_a_w[l].astype(BF16), gate_a_b[l].reshape(-1),
                   gate_x_w[l].astype(BF16), gate_x_b[l].reshape(-1), lru_lambda[l],
                   w_out[l].astype(BF16))
        q = _norm_matmul(h, norm2_g[l], peer_wq[l].astype(BF16))
        expert, gate = _topk(q, peer_keys1[l].astype(BF16), peer_keys2[l].astype(BF16))
        idx_flat = jnp.pad(expert.T.reshape(-1), (0, PEER_LOOKAHEAD * npairs))
        rpp = d_model // LANES
        gate_rep = jnp.repeat(gate.T, rpp, axis=1).reshape(-1, npairs * rpp // LANES, LANES)
        table = _pack_expert_table(peer_u[l], peer_v[l])
        h = _peer(idx_flat, gate_rep, h, norm2_g[l], norm_f_g, table)
    return h.reshape(batch, seq, d_model)
```

```python
import functools

import jax
import jax.numpy as jnp
from jax import lax
from jax.experimental import pallas as pl
from jax.experimental.pallas import tpu as pltpu

EPS = 1e-6
POOL_WINDOWS = (2, 4, 8, 16)
POOL_TAIL = 16
CONV_WIDTH = 4
CONV_TAIL = 8
LRU_C = 8.0
LRU_HEADS = 8
PEER_HEADS = 8
PEER_TOPK = 16
PEER_N_KEYS = 128

LANES = 128
SUBLANES = 8

F32 = jnp.float32
BF16 = jnp.bfloat16


def _rmsnorm(x, g):
    return x * lax.rsqrt(jnp.mean(x * x, axis=-1, keepdims=True) + EPS) * g


def _gelu_tanh(x):
    c = 0.7978845608028654
    return x * (0.5 * (1.0 + jnp.tanh(c * (x + 0.044715 * (x * x * x)))))


def _norm_matmul_kernel(x_ref, g_ref, w_ref, o_ref):
    z = _rmsnorm(x_ref[...], g_ref[...])
    o_ref[...] = jnp.dot(z.astype(BF16), w_ref[...], preferred_element_type=F32)


def _norm_matmul(x, g, w, *, tm=512, tn=1024):
    m, d = x.shape
    n = w.shape[1]
    return pl.pallas_call(
        _norm_matmul_kernel,
        grid=(n // tn, m // tm),
        in_specs=[
            pl.BlockSpec((tm, d), lambda j, i: (i, 0)),
            pl.BlockSpec((1, d), lambda j, i: (0, 0)),
            pl.BlockSpec((d, tn), lambda j, i: (0, j)),
        ],
        out_specs=pl.BlockSpec((tm, tn), lambda j, i: (i, j)),
        out_shape=jax.ShapeDtypeStruct((m, n), F32),
        compiler_params=pltpu.CompilerParams(
            dimension_semantics=("arbitrary", "arbitrary"),
            vmem_limit_bytes=40 * 1024 * 1024),
        name="norm_matmul",
    )(x, g.reshape(1, d), w)


def _mixer_kernel(up_ref, xl_ref, gl_ref, x_ref, pool_w_ref, pool_b_ref, pool_s_ref,
                  conv_w_ref, conv_b_ref, ga_w_ref, ga_b_ref, gx_w_ref, gx_b_ref,
                  lam_ref, w_out_ref, o_ref,
                  pool_tail, conv_tail, h_carry, a_scr, b_scr, h_scr):
    ts = up_ref.shape[0]
    pool_width = up_ref.shape[1]
    lru_width = xl_ref.shape[1]
    pool_group = pool_width // len(POOL_WINDOWS)
    head_dim = lru_width // LRU_HEADS
    s = pl.program_id(1)

    @pl.when(s == 0)
    def _():
        pool_tail[...] = jnp.zeros_like(pool_tail)
        conv_tail[...] = jnp.zeros_like(conv_tail)
        h_carry[...] = jnp.zeros_like(h_carry)

    u = up_ref[...]
    ext = jnp.concatenate([pool_tail[...], u], axis=0)
    pool_tail[...] = u[ts - POOL_TAIL:, :]
    pos = (s * ts + 1 + lax.broadcasted_iota(jnp.int32, (ts, 1), 0)).astype(F32)
    ys = []
    for g, w in enumerate(POOL_WINDOWS):
        acc = ext[:, g * pool_group:(g + 1) * pool_group]
        width = 1
        while width < w:
            acc = acc[width:, :] + acc[:-width, :]
            width *= 2
        start = POOL_TAIL + 1 - w
        win = acc[start:start + ts, :]
        d = win / jnp.minimum(pos, float(w)) - u[:, g * pool_group:(g + 1) * pool_group]
        ys.append(jnp.dot(d.astype(BF16), pool_w_ref[g], preferred_element_type=F32))
    y_pool = (jnp.concatenate(ys, axis=1) + pool_b_ref[...]) * pool_s_ref[...]

    xb = xl_ref[...]
    extx = jnp.concatenate([conv_tail[...], xb], axis=0)
    conv_tail[...] = xb[ts - CONV_TAIL:, :]
    xc = jnp.broadcast_to(conv_b_ref[...], xb.shape)
    for k in range(CONV_WIDTH):
        off = CONV_TAIL - (CONV_WIDTH - 1) + k
        xc = xc + extx[off:off + ts, :] * conv_w_ref[k:k + 1, :]
    xcb = xc.astype(BF16)
    rs, is_ = [], []
    for h in range(LRU_HEADS):
        xh = xcb[:, h * head_dim:(h + 1) * head_dim]
        rs.append(jnp.dot(xh, ga_w_ref[h], preferred_element_type=F32))
        is_.append(jnp.dot(xh, gx_w_ref[h], preferred_element_type=F32))
    r = jax.nn.sigmoid(jnp.concatenate(rs, axis=1) + ga_b_ref[...])
    i_gate = jax.nn.sigmoid(jnp.concatenate(is_, axis=1) + gx_b_ref[...])
    lam = lam_ref[...]
    log_sig = jnp.minimum(lam, 0.0) - jnp.log1p(jnp.exp(-jnp.abs(lam)))
    log_a = (LRU_C * r) * log_sig
    a_scr[...] = jnp.exp(log_a)
    th = jnp.tanh(log_a)
    b_scr[...] = jnp.sqrt(-2.0 * th / (1.0 - th)) * (i_gate * xc)

    row = lax.broadcasted_iota(jnp.int32, (SUBLANES, lru_width), 0)

    def scan_group(gi, h_prev):
        r0 = pl.multiple_of(gi * SUBLANES, SUBLANES)
        a = a_scr[pl.ds(r0, SUBLANES), :]
        b = b_scr[pl.ds(r0, SUBLANES), :]
        for d in (1, 2, 4):
            a_sh = jnp.where(row >= d, pltpu.roll(a, d, axis=0), 1.0)
            b_sh = jnp.where(row >= d, pltpu.roll(b, d, axis=0), 0.0)
            b = a * b_sh + b
            a = a * a_sh
        hg = a * h_prev + b
        h_scr[pl.ds(r0, SUBLANES), :] = hg
        return hg[SUBLANES - 1:SUBLANES, :]

    h_last = lax.fori_loop(0, ts // SUBLANES, scan_group, h_carry[0:1, :])
    h_carry[0:1, :] = h_last
    y_lru = h_scr[...] * _gelu_tanh(gl_ref[...])

    y = jnp.concatenate([y_pool, y_lru], axis=1).astype(BF16)
    o_ref[...] = x_ref[...] + jnp.dot(y, w_out_ref[...], preferred_element_type=F32)


def _mixer(proj, x2d, batch, seq, pool_w, pool_b, pool_scale, conv_w, conv_b,
           ga_w, ga_b, gx_w, gx_b, lam, w_out, *, ts=256):
    t, d_model = x2d.shape
    pool_width = pool_b.shape[-1]
    lru_width = lam.shape[-1]
    ns = seq // ts
    row_blk = lambda c: (lambda b, s: (b * ns + s, c))
    full = lambda a: pl.BlockSpec(a.shape, lambda b, s: (0,) * a.ndim)
    consts = [pool_w, pool_b.reshape(1, -1), pool_scale.reshape(1, -1), conv_w,
              conv_b.reshape(1, -1), ga_w, ga_b.reshape(1, -1), gx_w, gx_b.reshape(1, -1),
              lam.reshape(1, -1), w_out]
    return pl.pallas_call(
        _mixer_kernel,
        grid=(batch, ns),
        in_specs=[
            pl.BlockSpec((ts, pool_width), row_blk(0)),
            pl.BlockSpec((ts, lru_width), row_blk(pool_width // lru_width)),
            pl.BlockSpec((ts, lru_width), row_blk(pool_width // lru_width + 1)),
            pl.BlockSpec((ts, d_model), row_blk(0)),
        ] + [full(a) for a in consts],
        out_specs=pl.BlockSpec((ts, d_model), row_blk(0)),
        out_shape=jax.ShapeDtypeStruct((t, d_model), F32),
        scratch_shapes=[
            pltpu.VMEM((POOL_TAIL, pool_width), F32),
            pltpu.VMEM((CONV_TAIL, lru_width), F32),
            pltpu.VMEM((SUBLANES, lru_width), F32),
            pltpu.VMEM((ts, lru_width), F32),
            pltpu.VMEM((ts, lru_width), F32),
            pltpu.VMEM((ts, lru_width), F32),
        ],
        compiler_params=pltpu.CompilerParams(
            dimension_semantics=("arbitrary", "arbitrary"),
            vmem_limit_bytes=48 * 1024 * 1024),
        name="mixer",
    )(proj, proj, proj, x2d, *consts)


def _topk_rows(s, k, payload=None):
    n = s.shape[0]
    row = lax.broadcasted_iota(jnp.int32, s.shape, 0).astype(F32)
    vals, idxs, pays = [], [], []
    for _ in range(k):
        m = jnp.max(s, axis=0, keepdims=True)
        idx = jnp.min(jnp.where(s == m, row, float(n)), axis=0, keepdims=True)
        hit = row == idx
        vals.append(m)
        idxs.append(idx)
        if payload is not None:
            pays.append(jnp.sum(jnp.where(hit, payload, 0), axis=0, keepdims=True))
        s = jnp.where(hit, -jnp.inf, s)
    out = (jnp.concatenate(vals, axis=0), jnp.concatenate(idxs, axis=0).astype(jnp.int32))
    if payload is not None:
        out += (jnp.concatenate(pays, axis=0),)
    return out


def _pair_candidates(a_vals, b_vals, k):
    blocks = []
    a = 0
    while k // (a + 1) > 1:
        nb = min(k, -(-(k // (a + 1)) // SUBLANES) * SUBLANES)
        blocks.append(a_vals[a:a + 1, :] + b_vals[0:nb, :])
        a += 1
    blocks.append(a_vals[a:k, :] + b_vals[0:1, :])
    return jnp.concatenate(blocks, axis=0)


def _topk_kernel(q_ref, k1_ref, k2_ref, exp_ref, gate_ref):
    half = k1_ref.shape[2]
    nt = (((1,), (1,)), ((), ()))

    def head(h, carry):
        c0 = pl.multiple_of(h * 2 * half, 2 * half)
        q1 = q_ref[:, pl.ds(c0, half)].astype(BF16)
        q2 = q_ref[:, pl.ds(c0 + half, half)].astype(BF16)
        s1 = lax.dot_general(k1_ref[h], q1, nt, preferred_element_type=F32)
        s2 = lax.dot_general(k2_ref[h], q2, nt, preferred_element_type=F32)
        v1, i1 = _topk_rows(s1, PEER_TOPK)
        v2, i2 = _topk_rows(s2, PEER_TOPK)
        cand = _pair_candidates(v1, v2, PEER_TOPK)
        cidx = _pair_candidates(i1 * PEER_N_KEYS, i2, PEER_TOPK)
        top_s, _, expert = _topk_rows(cand, PEER_TOPK, payload=cidx)
        e = jnp.exp(top_s - jnp.max(top_s, axis=0, keepdims=True))
        gate = e / jnp.sum(e, axis=0, keepdims=True)
        r0 = pl.multiple_of(h * PEER_TOPK, PEER_TOPK)
        exp_ref[pl.ds(r0, PEER_TOPK), :] = expert
        gate_ref[pl.ds(r0, PEER_TOPK), :] = gate
        return carry

    lax.fori_loop(0, PEER_HEADS, head, 0)


def _topk(q, k1, k2, *, tb=512):
    t, dq = q.shape
    npairs = PEER_HEADS * PEER_TOPK
    return pl.pallas_call(
        _topk_kernel,
        grid=(t // tb,),
        in_specs=[
            pl.BlockSpec((tb, dq), lambda i: (i, 0)),
            pl.BlockSpec(k1.shape, lambda i: (0, 0, 0)),
            pl.BlockSpec(k2.shape, lambda i: (0, 0, 0)),
        ],
        out_specs=[
            pl.BlockSpec((npairs, tb), lambda i: (0, i)),
            pl.BlockSpec((npairs, tb), lambda i: (0, i)),
        ],
        out_shape=[
            jax.ShapeDtypeStruct((npairs, t), jnp.int32),
            jax.ShapeDtypeStruct((npairs, t), F32),
        ],
        compiler_params=pltpu.CompilerParams(dimension_semantics=("arbitrary",)),
        name="peer_topk",
    )(q, k1, k2)


PEER_GROUP = 8
PEER_PHASES = 4
PEER_SLOTS = PEER_GROUP * PEER_PHASES
PEER_DMA_QUEUES = 2
PEER_LOOKAHEAD = PEER_GROUP * PEER_PHASES


def _peer_kernel(idx_hbm, gate_ref, h_ref, g2_ref, gf_ref, tab_hbm, o_ref,
                 idx_smem, idx_sems, *scratch):
    tbp = h_ref.shape[0]
    bufs = scratch[:PEER_PHASES]
    sems, z_scr, p_scr = scratch[PEER_PHASES:]
    npairs, nchunk = bufs[0].shape[1] - 1, bufs[0].shape[2]
    uchunks = nchunk // 2
    rpp = 2 * uchunks
    assert LANES % rpp == 0 and npairs * rpp % LANES == 0
    ppt = LANES // rpp
    ntile = npairs // ppt
    blk = tbp * npairs
    win = (tbp + PEER_LOOKAHEAD) * npairs
    ahead = PEER_PHASES - 1
    ngroups = tbp // PEER_GROUP
    assert ngroups % PEER_PHASES == 0
    step = pl.program_id(0)
    nsteps = pl.num_programs(0)
    base = lax.rem(step, 2) * win

    def idx_copy(s):
        half = lax.rem(s, 2)
        return pltpu.make_async_copy(
            idx_hbm.at[pl.ds(pl.multiple_of(s * blk, blk), win)],
            idx_smem.at[pl.ds(pl.multiple_of(half * win, win), win)],
            idx_sems.at[half])

    def issue(t, phase, i):
        for k in range(npairs):
            e = idx_smem[base + t * npairs + k]
            pltpu.make_async_copy(tab_hbm.at[e], bufs[phase].at[i, k],
                                  sems.at[phase * PEER_GROUP + i]).start(priority=k % PEER_DMA_QUEUES)

    def wait(phase, i):
        rows = bufs[phase].at[i, pl.ds(0, npairs)]
        pltpu.make_async_copy(rows, rows, sems.at[phase * PEER_GROUP + i]).wait()

    def issue_group(g, phase):
        for i in range(PEER_GROUP):
            issue(g * PEER_GROUP + i, phase, i)

    def wait_group(phase):
        for i in range(PEER_GROUP):
            wait(phase, i)

    @pl.when(step == 0)
    def _():
        idx_copy(step).start()

    idx_copy(step).wait()

    @pl.when(step + 1 < nsteps)
    def _():
        idx_copy(step + 1).start()

    @pl.when(step == 0)
    def _():
        for buf in bufs:
            buf[0, npairs, 0:SUBLANES, :] = jnp.zeros((SUBLANES, LANES), jnp.uint32)
        for g0 in range(ahead):
            issue_group(g0, g0)

    def copy_order_pin(phase):
        return pltpu.bitcast(bufs[phase][0, npairs, 0:SUBLANES, :], F32)

    h1 = h_ref[...]
    z_scr[...] = _rmsnorm(h1, g2_ref[...])
    half_w = uchunks * LANES
    nt = (((1,), (1,)), ((), ()))
    lane = lax.broadcasted_iota(jnp.int32, (rpp, LANES), 1)
    diag = lax.broadcasted_iota(jnp.int32, (rpp, LANES), 0) == (lane & (rpp - 1))

    def split(x):
        hi = x.astype(BF16)
        return hi, (x - hi.astype(F32)).astype(BF16)

    def feature_start(r):
        c, p = divmod(r, 2)
        return p * half_w + c * LANES

    def part_view(rows, first):
        words = rows[0:npairs, first:first + uchunks, :]
        return pltpu.bitcast(words.reshape(npairs * uchunks, LANES), BF16)

    def neuron_inputs(t, rows):
        zrow = z_scr[pl.ds(t, 1), :]
        zr = jnp.concatenate([zrow[:, feature_start(r):feature_start(r) + LANES] for r in range(rpp)], axis=0)
        lhs = jnp.concatenate(split(zr), axis=0)
        return lax.dot_general(lhs, part_view(rows, 0), nt, preferred_element_type=F32)

    def neuron_weights(t, acc, pin=None):
        tot = acc[0:rpp] + acc[rpp:2 * rpp]
        s = jnp.concatenate([jnp.sum(jnp.where(diag, tot[:, j * LANES:(j + 1) * LANES], 0.0), axis=0, keepdims=True)
                             for j in range(ntile)], axis=0)
        if pin is not None:
            s = s + jnp.concatenate([pin] * (ntile // SUBLANES), axis=0)
        lane_s = lax.broadcasted_iota(jnp.int32, s.shape, 1)
        d = 1
        while d < rpp:
            s = s + jnp.where((lane_s & d) == 0, pltpu.roll(s, LANES - d, axis=1), pltpu.roll(s, d, axis=1))
            d *= 2
        w = gate_ref[t] * _gelu_tanh(s)
        wh = w.astype(BF16).astype(F32)
        wl = w - wh
        blocks = []
        for j in range(ntile):
            hi = jnp.where(diag, jnp.broadcast_to(wh[j:j + 1, :], diag.shape), 0.0)
            lo = jnp.where(diag, jnp.broadcast_to(wl[j:j + 1, :], diag.shape), 0.0)
            blocks.append(jnp.concatenate([hi, lo], axis=0).astype(BF16))
        return jnp.concatenate(blocks, axis=1)

    def neuron_outputs(t, rows, lhs, pin=None):
        res = jnp.dot(lhs, part_view(rows, uchunks), preferred_element_type=F32)
        tot = res[0:rpp] + res[rpp:2 * rpp]
        if pin is not None:
            tot = tot + jnp.concatenate([pin] * (rpp // SUBLANES), axis=0)
        order = sorted(range(rpp), key=feature_start)
        p_scr[pl.ds(t, 1), :] = jnp.concatenate([tot[r:r + 1, :] for r in order], axis=1)

    def round_(r, carry):
        for phase in range(PEER_PHASES):
            g = r * PEER_PHASES + phase
            wait_group(phase)
            toks = [(g * PEER_GROUP + i, bufs[phase].at[i]) for i in range(PEER_GROUP)]
            nxt = (phase + ahead) % PEER_PHASES
            early = PEER_GROUP // 2
            w_pins, o_pins = [None] * PEER_GROUP, [None] * PEER_GROUP
            for i in range(PEER_GROUP):
                issue((g + ahead) * PEER_GROUP + i, nxt, i)
                if i < early:
                    w_pins[i] = copy_order_pin(nxt)
                elif i + 1 < PEER_GROUP:
                    o_pins[i - early] = copy_order_pin(nxt)
            accs = [neuron_inputs(t, rows) for t, rows in toks]
            lhss = [neuron_weights(t, acc, pin) for (t, _), acc, pin in zip(toks, accs, w_pins)]
            for (t, rows), lhs, pin in zip(toks, lhss, o_pins):
                neuron_outputs(t, rows, lhs, pin)
        return carry

    lax.fori_loop(0, ngroups // PEER_PHASES, round_, 0)

    @pl.when(step == nsteps - 1)
    def _():
        for g0 in range(ahead):
            wait_group(g0)

    o_ref[...] = _rmsnorm(h1 + p_scr[...], gf_ref[...])


def _peer(idx_flat, gate_rep, h1, g2, gf, table, *, tbp=128):
    t, d = h1.shape
    npairs = PEER_HEADS * PEER_TOPK
    nchunk = table.shape[1]
    return pl.pallas_call(
        _peer_kernel,
        grid=(t // tbp,),
        in_specs=[
            pl.BlockSpec(memory_space=pl.ANY),
            pl.BlockSpec((tbp,) + gate_rep.shape[1:], lambda i: (i, 0, 0)),
            pl.BlockSpec((tbp, d), lambda i: (i, 0)),
            pl.BlockSpec((1, d), lambda i: (0, 0)),
            pl.BlockSpec((1, d), lambda i: (0, 0)),
            pl.BlockSpec(memory_space=pl.ANY),
        ],
        out_specs=pl.BlockSpec((tbp, d), lambda i: (i, 0)),
        out_shape=jax.ShapeDtypeStruct((t, d), F32),
        scratch_shapes=[
            pltpu.SMEM((2 * (tbp + PEER_LOOKAHEAD) * npairs,), jnp.int32),
            pltpu.SemaphoreType.DMA((2,)),
        ] + [pltpu.VMEM((PEER_GROUP, npairs + 1, nchunk, LANES), jnp.uint32)] * PEER_PHASES + [
            pltpu.SemaphoreType.DMA((PEER_SLOTS,)),
            pltpu.VMEM((tbp, d), F32),
            pltpu.VMEM((tbp, d), F32),
        ],
        compiler_params=pltpu.CompilerParams(
            dimension_semantics=("arbitrary",),
            vmem_limit_bytes=56 * 1024 * 1024),
        name="peer_experts",
    )(idx_flat, gate_rep, h1, g2.reshape(1, d), gf.reshape(1, d), table)


def _pack_expert_table(peer_u, peer_v):
    n, d = peer_u.shape

    def words(w):
        b = lax.bitcast_convert_type(w.astype(BF16), jnp.uint16).astype(jnp.uint32)
        return b[:, :d // 2] | (b[:, d // 2:] << 16)

    return jnp.concatenate([words(peer_u), words(peer_v)], axis=1).reshape(n, d // LANES, LANES)


def kernel(x, norm1_g, w_in, pool_w, pool_b, pool_scale, conv_w, conv_b, gate_a_w, gate_a_b,
           gate_x_w, gate_x_b, lru_lambda, w_out, norm2_g, peer_wq, peer_keys1, peer_keys2,
           peer_u, peer_v, norm_f_g):
    batch, seq, d_model = x.shape
    assert w_in.shape[0] == 1, "the final norm is fused into the single layer's PEER kernel"
    npairs = PEER_HEADS * PEER_TOPK
    h = x.reshape(batch * seq, d_model)
    for l in range(1):
        proj = _norm_matmul(h, norm1_g[l], w_in[l].astype(BF16))
        h = _mixer(proj, h, batch, seq, pool_w[l].astype(BF16), pool_b[l].reshape(-1), pool_scale[l],
                   conv_w[l], conv_b[l], gate_a_w[l].astype(BF16), gate_a_b[l].reshape(-1),
                   gate_x_w[l].astype(BF16), gate_x_b[l].reshape(-1), lru_lambda[l],
                   w_out[l].astype(BF16))
        q = _norm_matmul(h, norm2_g[l], peer_wq[l].astype(BF16))
        expert, gate = _topk(q, peer_keys1[l].astype(BF16), peer_keys2[l].astype(BF16))
        idx_flat = jnp.pad(expert.T.reshape(-1), (0, PEER_LOOKAHEAD * npairs))
        rpp = d_model // LANES
        gate_rep = jnp.repeat(gate.T, rpp, axis=1).reshape(-1, npairs * rpp // LANES, LANES)
        table = _pack_expert_table(peer_u[l], peer_v[l])
        h = _peer(idx_flat, gate_rep, h, norm2_g[l], norm_f_g, table)
    return h.reshape(batch, seq, d_model)
```

```python
import functools

import jax
import jax.numpy as jnp
from jax import lax
from jax.experimental import pallas as pl
from jax.experimental.pallas import tpu as pltpu

EPS = 1e-6
POOL_WINDOWS = (2, 4, 8, 16)
POOL_TAIL = 16
CONV_WIDTH = 4
CONV_TAIL = 8
LRU_C = 8.0
LRU_HEADS = 8
PEER_HEADS = 8
PEER_TOPK = 16
PEER_N_KEYS = 128

LANES = 128
SUBLANES = 8

F32 = jnp.float32
BF16 = jnp.bfloat16


def _rmsnorm(x, g):
    return x * lax.rsqrt(jnp.mean(x * x, axis=-1, keepdims=True) + EPS) * g


def _gelu_tanh(x):
    c = 0.7978845608028654
    return x * (0.5 * (1.0 + jnp.tanh(c * (x + 0.044715 * (x * x * x)))))


def _norm_matmul_kernel(x_ref, g_ref, w_ref, o_ref):
    z = _rmsnorm(x_ref[...], g_ref[...])
    o_ref[...] = jnp.dot(z.astype(BF16), w_ref[...], preferred_element_type=F32)


def _norm_matmul(x, g, w, *, tm=512, tn=1024):
    m, d = x.shape
    n = w.shape[1]
    return pl.pallas_call(
        _norm_matmul_kernel,
        grid=(n // tn, m // tm),
        in_specs=[
            pl.BlockSpec((tm, d), lambda j, i: (i, 0)),
            pl.BlockSpec((1, d), lambda j, i: (0, 0)),
            pl.BlockSpec((d, tn), lambda j, i: (0, j)),
        ],
        out_specs=pl.BlockSpec((tm, tn), lambda j, i: (i, j)),
        out_shape=jax.ShapeDtypeStruct((m, n), F32),
        compiler_params=pltpu.CompilerParams(
            dimension_semantics=("arbitrary", "arbitrary"),
            vmem_limit_bytes=40 * 1024 * 1024),
        name="norm_matmul",
    )(x, g.reshape(1, d), w)


def _mixer_kernel(up_ref, xl_ref, gl_ref, x_ref, pool_w_ref, pool_b_ref, pool_s_ref,
                  conv_w_ref, conv_b_ref, ga_w_ref, ga_b_ref, gx_w_ref, gx_b_ref,
                  lam_ref, w_out_ref, o_ref,
                  pool_tail, conv_tail, h_carry, a_scr, b_scr, h_scr):
    ts = up_ref.shape[0]
    pool_width = up_ref.shape[1]
    lru_width = xl_ref.shape[1]
    pool_group = pool_width // len(POOL_WINDOWS)
    head_dim = lru_width // LRU_HEADS
    s = pl.program_id(1)

    @pl.when(s == 0)
    def _():
        pool_tail[...] = jnp.zeros_like(pool_tail)
        conv_tail[...] = jnp.zeros_like(conv_tail)
        h_carry[...] = jnp.zeros_like(h_carry)

    u = up_ref[...]
    ext = jnp.concatenate([pool_tail[...], u], axis=0)
    pool_tail[...] = u[ts - POOL_TAIL:, :]
    pos = (s * ts + 1 + lax.broadcasted_iota(jnp.int32, (ts, 1), 0)).astype(F32)
    ys = []
    for g, w in enumerate(POOL_WINDOWS):
        acc = ext[:, g * pool_group:(g + 1) * pool_group]
        width = 1
        while width < w:
            acc = acc[width:, :] + acc[:-width, :]
            width *= 2
        start = POOL_TAIL + 1 - w
        win = acc[start:start + ts, :]
        d = win / jnp.minimum(pos, float(w)) - u[:, g * pool_group:(g + 1) * pool_group]
        ys.append(jnp.dot(d.astype(BF16), pool_w_ref[g], preferred_element_type=F32))
    y_pool = (jnp.concatenate(ys, axis=1) + pool_b_ref[...]) * pool_s_ref[...]

    xb = xl_ref[...]
    extx = jnp.concatenate([conv_tail[...], xb], axis=0)
    conv_tail[...] = xb[ts - CONV_TAIL:, :]
    xc = jnp.broadcast_to(conv_b_ref[...], xb.shape)
    for k in range(CONV_WIDTH):
        off = CONV_TAIL - (CONV_WIDTH - 1) + k
        xc = xc + extx[off:off + ts, :] * conv_w_ref[k:k + 1, :]
    xcb = xc.astype(BF16)
    rs, is_ = [], []
    for h in range(LRU_HEADS):
        xh = xcb[:, h * head_dim:(h + 1) * head_dim]
        rs.append(jnp.dot(xh, ga_w_ref[h], preferred_element_type=F32))
        is_.append(jnp.dot(xh, gx_w_ref[h], preferred_element_type=F32))
    r = jax.nn.sigmoid(jnp.concatenate(rs, axis=1) + ga_b_ref[...])
    i_gate = jax.nn.sigmoid(jnp.concatenate(is_, axis=1) + gx_b_ref[...])
    lam = lam_ref[...]
    log_sig = jnp.minimum(lam, 0.0) - jnp.log1p(jnp.exp(-jnp.abs(lam)))
    log_a = (LRU_C * r) * log_sig
    a_scr[...] = jnp.exp(log_a)
    th = jnp.tanh(log_a)
    b_scr[...] = jnp.sqrt(-2.0 * th / (1.0 - th)) * (i_gate * xc)

    row = lax.broadcasted_iota(jnp.int32, (SUBLANES, lru_width), 0)

    def scan_group(gi, h_prev):
        r0 = pl.multiple_of(gi * SUBLANES, SUBLANES)
        a = a_scr[pl.ds(r0, SUBLANES), :]
        b = b_scr[pl.ds(r0, SUBLANES), :]
        for d in (1, 2, 4):
            a_sh = jnp.where(row >= d, pltpu.roll(a, d, axis=0), 1.0)
            b_sh = jnp.where(row >= d, pltpu.roll(b, d, axis=0), 0.0)
            b = a * b_sh + b
            a = a * a_sh
        hg = a * h_prev + b
        h_scr[pl.ds(r0, SUBLANES), :] = hg
        return hg[SUBLANES - 1:SUBLANES, :]

    h_last = lax.fori_loop(0, ts // SUBLANES, scan_group, h_carry[0:1, :])
    h_carry[0:1, :] = h_last
    y_lru = h_scr[...] * _gelu_tanh(gl_ref[...])

    y = jnp.concatenate([y_pool, y_lru], axis=1).astype(BF16)
    o_ref[...] = x_ref[...] + jnp.dot(y, w_out_ref[...], preferred_element_type=F32)


def _mixer(proj, x2d, batch, seq, pool_w, pool_b, pool_scale, conv_w, conv_b,
           ga_w, ga_b, gx_w, gx_b, lam, w_out, *, ts=256):
    t, d_model = x2d.shape
    pool_width = pool_b.shape[-1]
    lru_width = lam.shape[-1]
    ns = seq // ts
    row_blk = lambda c: (lambda b, s: (b * ns + s, c))
    full = lambda a: pl.BlockSpec(a.shape, lambda b, s: (0,) * a.ndim)
    consts = [pool_w, pool_b.reshape(1, -1), pool_scale.reshape(1, -1), conv_w,
              conv_b.reshape(1, -1), ga_w, ga_b.reshape(1, -1), gx_w, gx_b.reshape(1, -1),
              lam.reshape(1, -1), w_out]
    return pl.pallas_call(
        _mixer_kernel,
        grid=(batch, ns),
        in_specs=[
            pl.BlockSpec((ts, pool_width), row_blk(0)),
            pl.BlockSpec((ts, lru_width), row_blk(pool_width // lru_width)),
            pl.BlockSpec((ts, lru_width), row_blk(pool_width // lru_width + 1)),
            pl.BlockSpec((ts, d_model), row_blk(0)),
        ] + [full(a) for a in consts],
        out_specs=pl.BlockSpec((ts, d_model), row_blk(0)),
        out_shape=jax.ShapeDtypeStruct((t, d_model), F32),
        scratch_shapes=[
            pltpu.VMEM((POOL_TAIL, pool_width), F32),
            pltpu.VMEM((CONV_TAIL, lru_width), F32),
            pltpu.VMEM((SUBLANES, lru_width), F32),
            pltpu.VMEM((ts, lru_width), F32),
            pltpu.VMEM((ts, lru_width), F32),
            pltpu.VMEM((ts, lru_width), F32),
        ],
        compiler_params=pltpu.CompilerParams(
            dimension_semantics=("arbitrary", "arbitrary"),
            vmem_limit_bytes=48 * 1024 * 1024),
        name="mixer",
    )(proj, proj, proj, x2d, *consts)


def _topk_rows(s, k, payload=None):
    n = s.shape[0]
    row = lax.broadcasted_iota(jnp.int32, s.shape, 0).astype(F32)
    vals, idxs, pays = [], [], []
    for _ in range(k):
        m = jnp.max(s, axis=0, keepdims=True)
        idx = jnp.min(jnp.where(s == m, row, float(n)), axis=0, keepdims=True)
        hit = row == idx
        vals.append(m)
        idxs.append(idx)
        if payload is not None:
            pays.append(jnp.sum(jnp.where(hit, payload, 0), axis=0, keepdims=True))
        s = jnp.where(hit, -jnp.inf, s)
    out = (jnp.concatenate(vals, axis=0), jnp.concatenate(idxs, axis=0).astype(jnp.int32))
    if payload is not None:
        out += (jnp.concatenate(pays, axis=0),)
    return out


def _pair_candidates(a_vals, b_vals, k):
    blocks = []
    a = 0
    while k // (a + 1) > 1:
        nb = min(k, -(-(k // (a + 1)) // SUBLANES) * SUBLANES)
        blocks.append(a_vals[a:a + 1, :] + b_vals[0:nb, :])
        a += 1
    blocks.append(a_vals[a:k, :] + b_vals[0:1, :])
    return jnp.concatenate(blocks, axis=0)


def _topk_kernel(q_ref, k1_ref, k2_ref, exp_ref, gate_ref):
    half = k1_ref.shape[2]
    nt = (((1,), (1,)), ((), ()))

    def head(h, carry):
        c0 = pl.multiple_of(h * 2 * half, 2 * half)
        q1 = q_ref[:, pl.ds(c0, half)].astype(BF16)
        q2 = q_ref[:, pl.ds(c0 + half, half)].astype(BF16)
        s1 = lax.dot_general(k1_ref[h], q1, nt, preferred_element_type=F32)
        s2 = lax.dot_general(k2_ref[h], q2, nt, preferred_element_type=F32)
        v1, i1 = _topk_rows(s1, PEER_TOPK)
        v2, i2 = _topk_rows(s2, PEER_TOPK)
        cand = _pair_candidates(v1, v2, PEER_TOPK)
        cidx = _pair_candidates(i1 * PEER_N_KEYS, i2, PEER_TOPK)
        top_s, _, expert = _topk_rows(cand, PEER_TOPK, payload=cidx)
        e = jnp.exp(top_s - jnp.max(top_s, axis=0, keepdims=True))
        gate = e / jnp.sum(e, axis=0, keepdims=True)
        r0 = pl.multiple_of(h * PEER_TOPK, PEER_TOPK)
        exp_ref[pl.ds(r0, PEER_TOPK), :] = expert
        gate_ref[pl.ds(r0, PEER_TOPK), :] = gate
        return carry

    lax.fori_loop(0, PEER_HEADS, head, 0)


def _topk(q, k1, k2, *, tb=512):
    t, dq = q.shape
    npairs = PEER_HEADS * PEER_TOPK
    return pl.pallas_call(
        _topk_kernel,
        grid=(t // tb,),
        in_specs=[
            pl.BlockSpec((tb, dq), lambda i: (i, 0)),
            pl.BlockSpec(k1.shape, lambda i: (0, 0, 0)),
            pl.BlockSpec(k2.shape, lambda i: (0, 0, 0)),
        ],
        out_specs=[
            pl.BlockSpec((npairs, tb), lambda i: (0, i)),
            pl.BlockSpec((npairs, tb), lambda i: (0, i)),
        ],
        out_shape=[
            jax.ShapeDtypeStruct((npairs, t), jnp.int32),
            jax.ShapeDtypeStruct((npairs, t), F32),
        ],
        compiler_params=pltpu.CompilerParams(dimension_semantics=("arbitrary",)),
        name="peer_topk",
    )(q, k1, k2)


PEER_GROUP = 16
PEER_PHASES = 2
PEER_SLOTS = PEER_GROUP * PEER_PHASES
PEER_ISSUE_CHUNKS = 2 * PEER_GROUP
PEER_DMA_QUEUES = 2
PEER_LOOKAHEAD = PEER_GROUP * PEER_PHASES


def _peer_kernel(idx_hbm, gate_ref, h_ref, g2_ref, gf_ref, tab_hbm, o_ref,
                 idx_smem, idx_sems, *scratch):
    tbp = h_ref.shape[0]
    bufs = scratch[:PEER_PHASES]
    sems, z_scr, p_scr = scratch[PEER_PHASES:]
    npairs = bufs[0].shape[1] - 1
    rpp = bufs[0].shape[2] // 2
    assert LANES % rpp == 0 and npairs * rpp % LANES == 0
    ppt = LANES // rpp
    ntile = npairs // ppt
    blk = tbp * npairs
    win = (tbp + PEER_LOOKAHEAD) * npairs
    ahead = PEER_PHASES - 1
    ngroups = tbp // PEER_GROUP
    assert ngroups % PEER_PHASES == 0
    step = pl.program_id(0)
    nsteps = pl.num_programs(0)
    base = lax.rem(step, 2) * win

    def idx_copy(s):
        half = lax.rem(s, 2)
        return pltpu.make_async_copy(
            idx_hbm.at[pl.ds(pl.multiple_of(s * blk, blk), win)],
            idx_smem.at[pl.ds(pl.multiple_of(half * win, win), win)],
            idx_sems.at[half])

    def issue(t, phase, i, first=0, count=None):
        for k in range(first, first + (npairs if count is None else count)):
            e = idx_smem[base + t * npairs + k]
            pltpu.make_async_copy(tab_hbm.at[e], bufs[phase].at[i, k],
                                  sems.at[phase * PEER_GROUP + i]).start(priority=k % PEER_DMA_QUEUES)

    def wait(phase, i):
        rows = bufs[phase].at[i, pl.ds(0, npairs)]
        pltpu.make_async_copy(rows, rows, sems.at[phase * PEER_GROUP + i]).wait()

    def issue_group(g, phase):
        for i in range(PEER_GROUP):
            issue(g * PEER_GROUP + i, phase, i)

    def wait_group(phase):
        for i in range(PEER_GROUP):
            wait(phase, i)

    @pl.when(step == 0)
    def _():
        idx_copy(step).start()

    idx_copy(step).wait()

    @pl.when(step + 1 < nsteps)
    def _():
        idx_copy(step + 1).start()

    @pl.when(step == 0)
    def _():
        for buf in bufs:
            buf[0, npairs, 0:rpp, :] = jnp.zeros((rpp, LANES), BF16)
        for g0 in range(ahead):
            issue_group(g0, g0)

    def copy_order_pin(phase):
        return bufs[phase][0, npairs, 0:rpp, :].astype(F32)

    h1 = h_ref[...]
    z_scr[...] = _rmsnorm(h1, g2_ref[...])
    nt = (((1,), (1,)), ((), ()))
    lane = lax.broadcasted_iota(jnp.int32, (rpp, LANES), 1)
    diag = lax.broadcasted_iota(jnp.int32, (rpp, LANES), 0) == (lane & (rpp - 1))

    def split(x):
        hi = x.astype(BF16)
        return hi, (x - hi.astype(F32)).astype(BF16)

    def part_view(rows, first):
        return rows[0:npairs, first:first + rpp, :].reshape(npairs * rpp, LANES)

    def neuron_inputs(t, rows, pin=None):
        zrow = z_scr[pl.ds(t, 1), :]
        zr = jnp.concatenate([zrow[:, r * LANES:(r + 1) * LANES] for r in range(rpp)], axis=0)
        if pin is not None:
            zr = zr + pin
        lhs = jnp.concatenate(split(zr), axis=0)
        return lax.dot_general(lhs, part_view(rows, 0), nt, preferred_element_type=F32)

    def neuron_weights(t, acc):
        tot = acc[0:rpp] + acc[rpp:2 * rpp]
        s = jnp.concatenate([jnp.sum(jnp.where(diag, tot[:, j * LANES:(j + 1) * LANES], 0.0), axis=0, keepdims=True)
                             for j in range(ntile)], axis=0)
        lane_s = lax.broadcasted_iota(jnp.int32, s.shape, 1)
        d = 1
        while d < rpp:
            s = s + jnp.where((lane_s & d) == 0, pltpu.roll(s, LANES - d, axis=1), pltpu.roll(s, d, axis=1))
            d *= 2
        w = gate_ref[t] * _gelu_tanh(s)
        wh = w.astype(BF16).astype(F32)
        wl = w - wh
        blocks = []
        for j in range(ntile):
            hi = jnp.where(diag, jnp.broadcast_to(wh[j:j + 1, :], diag.shape), 0.0)
            lo = jnp.where(diag, jnp.broadcast_to(wl[j:j + 1, :], diag.shape), 0.0)
            blocks.append(jnp.concatenate([hi, lo], axis=0).astype(BF16))
        return jnp.concatenate(blocks, axis=1)

    def neuron_outputs(t, rows, lhs, pin=None):
        res = jnp.dot(lhs, part_view(rows, rpp), preferred_element_type=F32)
        tot = res[0:rpp] + res[rpp:2 * rpp]
        if pin is not None:
            tot = tot + pin
        p_scr[pl.ds(t, 1), :] = jnp.concatenate([tot[r:r + 1, :] for r in range(rpp)], axis=1)

    def round_(r, carry):
        for phase in range(PEER_PHASES):
            g = r * PEER_PHASES + phase
            wait_group(phase)
            toks = [(g * PEER_GROUP + i, bufs[phase].at[i]) for i in range(PEER_GROUP)]
            nxt = (phase + ahead) % PEER_PHASES
            per_tok = PEER_ISSUE_CHUNKS // PEER_GROUP
            size = npairs // per_tok
            n_in = PEER_GROUP - 1
            i_pins, o_pins = [None] * PEER_GROUP, [None] * PEER_GROUP
            for c in range(PEER_ISSUE_CHUNKS):
                i = c // per_tok
                issue((g + ahead) * PEER_GROUP + i, nxt, i, (c % per_tok) * size, size)
                if c < n_in:
                    i_pins[c + 1] = copy_order_pin(nxt)
                elif c - n_in < PEER_GROUP and c + 1 < PEER_ISSUE_CHUNKS:
                    o_pins[c - n_in] = copy_order_pin(nxt)
            accs = [neuron_inputs(t, rows, pin) for (t, rows), pin in zip(toks, i_pins)]
            lhss = [neuron_weights(t, acc) for (t, _), acc in zip(toks, accs)]
            for (t, rows), lhs, pin in zip(toks, lhss, o_pins):
                neuron_outputs(t, rows, lhs, pin)
        return carry

    lax.fori_loop(0, ngroups // PEER_PHASES, round_, 0)

    @pl.when(step == nsteps - 1)
    def _():
        for g0 in range(ahead):
            wait_group(g0)

    o_ref[...] = _rmsnorm(h1 + p_scr[...], gf_ref[...])


def _peer(idx_flat, gate_rep, h1, g2, gf, table, *, tbp=128):
    t, d = h1.shape
    npairs = PEER_HEADS * PEER_TOPK
    nrows = table.shape[1]
    return pl.pallas_call(
        _peer_kernel,
        grid=(t // tbp,),
        in_specs=[
            pl.BlockSpec(memory_space=pl.ANY),
            pl.BlockSpec((tbp,) + gate_rep.shape[1:], lambda i: (i, 0, 0)),
            pl.BlockSpec((tbp, d), lambda i: (i, 0)),
            pl.BlockSpec((1, d), lambda i: (0, 0)),
            pl.BlockSpec((1, d), lambda i: (0, 0)),
            pl.BlockSpec(memory_space=pl.ANY),
        ],
        out_specs=pl.BlockSpec((tbp, d), lambda i: (i, 0)),
        out_shape=jax.ShapeDtypeStruct((t, d), F32),
        scratch_shapes=[
            pltpu.SMEM((2 * (tbp + PEER_LOOKAHEAD) * npairs,), jnp.int32),
            pltpu.SemaphoreType.DMA((2,)),
        ] + [pltpu.VMEM((PEER_GROUP, npairs + 1, nrows, LANES), table.dtype)] * PEER_PHASES + [
            pltpu.SemaphoreType.DMA((PEER_SLOTS,)),
            pltpu.VMEM((tbp, d), F32),
            pltpu.VMEM((tbp, d), F32),
        ],
        compiler_params=pltpu.CompilerParams(
            dimension_semantics=("arbitrary",),
            vmem_limit_bytes=56 * 1024 * 1024),
        name="peer_experts",
    )(idx_flat, gate_rep, h1, g2.reshape(1, d), gf.reshape(1, d), table)


def _expert_table(peer_u, peer_v):
    n, d = peer_u.shape
    return jnp.concatenate([peer_u.astype(BF16).reshape(n, d // LANES, LANES),
                            peer_v.astype(BF16).reshape(n, d // LANES, LANES)], axis=1)


def kernel(x, norm1_g, w_in, pool_w, pool_b, pool_scale, conv_w, conv_b, gate_a_w, gate_a_b,
           gate_x_w, gate_x_b, lru_lambda, w_out, norm2_g, peer_wq, peer_keys1, peer_keys2,
           peer_u, peer_v, norm_f_g):
    batch, seq, d_model = x.shape
    assert w_in.shape[0] == 1, "the final norm is fused into the single layer's PEER kernel"
    npairs = PEER_HEADS * PEER_TOPK
    h = x.reshape(batch * seq, d_model)
    for l in range(1):
        proj = _norm_matmul(h, norm1_g[l], w_in[l].astype(BF16))
        h = _mixer(proj, h, batch, seq, pool_w[l].astype(BF16), pool_b[l].reshape(-1), pool_scale[l],
                   conv_w[l], conv_b[l], gate_a_w[l].astype(BF16), gate_a_b[l].reshape(-1),
                   gate_x_w[l].astype(BF16), gate_x_b[l].reshape(-1), lru_lambda[l],
                   w_out[l].astype(BF16))
        q = _norm_matmul(h, norm2_g[l], peer_wq[l].astype(BF16))
        expert, gate = _topk(q, peer_keys1[l].astype(BF16), peer_keys2[l].astype(BF16))
        idx_flat = jnp.pad(expert.T.reshape(-1), (0, PEER_LOOKAHEAD * npairs))
        rpp = d_model // LANES
        gate_rep = jnp.repeat(gate.T, rpp, axis=1).reshape(-1, npairs * rpp // LANES, LANES)
        table = _expert_table(peer_u[l], peer_v[l])
        h = _peer(idx_flat, gate_rep, h, norm2_g[l], norm_f_g, table)
    return h.reshape(batch, seq, d_model)
```
